```python
import math
import jax, jax.numpy as jnp
from jax import lax
import numpy as np

D_MODEL = 1024
BATCH = 2
SEQ = 8192
DEPTH = 1
DEC_BATCH = 8
DEC_SEQ = 8192
PAST_LEN = 128

F_WIDTH = D_MODEL
F_GROUPS = 4
F_GROUP_DIM = F_WIDTH // F_GROUPS
H_WIDTH = D_MODEL
H_ORDER = 2
N_DIR = 2
SHORT_CONV = 3
FILT_EMB = 33
FILT_BANDS = (FILT_EMB - 1) // 2
FILT_HIDDEN = 64
N_BRANCH = 2
RMS_EPS = 1e-6
DECAY_TARGET = 1e-2
FAST_DECAY_PCT = 0.3
SLOW_DECAY_PCT = 1.5
PROJ_COLS = 2 * F_WIDTH + (H_ORDER + 1) * H_WIDTH + H_WIDTH + N_BRANCH * D_MODEL

kernel_name = "fourier_hyena_gated_encoder"


def rmsnorm(x, g):
    xf = x.astype(jnp.float32)
    y = xf * lax.rsqrt(jnp.mean(xf * xf, axis=-1, keepdims=True) + RMS_EPS)
    return (y * g.astype(jnp.float32)).astype(x.dtype)


def short_conv(u, w):
    L = u.shape[1]
    up = jnp.pad(u, ((0, 0), (1, 1), (0, 0)))
    return sum(up[:, k:k + L] * w[k] for k in range(SHORT_CONV))


def hyena_filters(L, w1, b1, f1, w2, b2, f2, w3, decay):
    f32 = jnp.float32
    pos = jnp.arange(L, dtype=f32)
    t = pos / (L - 1)
    bands = jnp.linspace(1e-4, FILT_BANDS - 1, FILT_BANDS, dtype=f32)
    ang = (2.0 * math.pi * pos / L)[:, None] * bands[None]
    z = jnp.concatenate([t[:, None], jnp.cos(ang), -jnp.sin(ang)], axis=-1)
    h = jnp.sin(f1.astype(f32) * (z @ w1.astype(f32) + b1.astype(f32)))
    h = jnp.sin(f2.astype(f32) * (h @ w2.astype(f32) + b2.astype(f32)))
    h = (h @ w3.astype(f32)).reshape(L, N_DIR, H_ORDER, H_WIDTH)
    h = h * jnp.exp(-t[:, None, None, None] * jnp.abs(decay.astype(f32))[None])
    h = h / jnp.sum(jnp.abs(h), axis=(0, 1), keepdims=True)
    k = jnp.concatenate([h[:, 0],
                         jnp.zeros((1, H_ORDER, H_WIDTH), f32),
                         h[:L - 1, 1][::-1]], axis=0)
    return jnp.fft.rfft(k, axis=0)


def long_conv(u, kf, d):
    L = u.shape[1]
    uf = jnp.fft.rfft(u, n=2 * L, axis=1)
    y = jnp.fft.irfft(uf * kf[None], n=2 * L, axis=1)[:, :L]
    return y + d * u


def encoder_layer(x, g_pre, w_in, w_short, filt_w1, filt_b1, filt_freq1, filt_w2,
                  filt_b2, filt_freq2, filt_w3, filt_decay, hyena_d,
                  w_fourier_out, w_hyena_out, b_merge, w_out, g_post):
    B, L, _ = x.shape
    dt = x.dtype
    xn = rmsnorm(x, g_pre)
    p = xn @ w_in
    fv, fg, hv, hg, mg = jnp.split(
        p, [F_WIDTH, 2 * F_WIDTH, 2 * F_WIDTH + 3 * H_WIDTH,
            2 * F_WIDTH + 4 * H_WIDTH], axis=-1)

    fz = fv.astype(jnp.float32).reshape(B, L, F_GROUPS, F_GROUP_DIM)
    fz = jnp.fft.fftn(fz, axes=(1, 3), norm="ortho").real
    fz = fz.reshape(B, L, F_WIDTH).astype(dt)
    y_f = (fz * jax.nn.silu(fg)) @ w_fourier_out

    hv = short_conv(hv, w_short)
    v, x1, x2 = jnp.split(hv, 3, axis=-1)
    kf = hyena_filters(L, filt_w1, filt_b1, filt_freq1, filt_w2, filt_b2,
                       filt_freq2, filt_w3, filt_decay)
    z = v.astype(jnp.float32)
    for n, gate in enumerate((x1, x2)):
        z = gate.astype(jnp.float32) * long_conv(z, kf[:, n],
                                                 hyena_d[n].astype(jnp.float32))
    y_h = (z.astype(dt) * jax.nn.silu(hg)) @ w_hyena_out

    g_f, g_h = jnp.split(jax.nn.sigmoid(mg + b_merge), 2, axis=-1)
    out = (g_f * y_f + g_h * y_h) @ w_out
    return x + rmsnorm(out, g_post)


def setup_inputs(seed: int = 0) -> dict:
    key = jax.random.key(seed)
    ks = jax.random.split(key, 20)
    f32 = jnp.float32
    nrm = lambda k, shape, s: jax.random.normal(k, shape, f32) * s
    base_decay = jnp.abs(jnp.linspace(math.log(DECAY_TARGET) / FAST_DECAY_PCT,
                                      math.log(DECAY_TARGET) / SLOW_DECAY_PCT,
                                      H_WIDTH, dtype=f32))
    decay = jnp.broadcast_to(base_decay, (DEPTH, N_DIR, H_ORDER, H_WIDTH))
    decay = decay + nrm(ks[11], (DEPTH, N_DIR, H_ORDER, H_WIDTH), 0.1)
    return {
        "x_prompt": nrm(ks[0], (BATCH, SEQ, D_MODEL), 1.0),
        "x_sample": nrm(ks[1], (DEC_BATCH, DEC_SEQ, D_MODEL), 1.0),
        "g_pre": 1.0 + nrm(ks[2], (DEPTH, D_MODEL), 0.02),
        "w_in": nrm(ks[3], (DEPTH, D_MODEL, PROJ_COLS), D_MODEL ** -0.5),
        "w_short": nrm(ks[4], (DEPTH, SHORT_CONV, 3 * H_WIDTH), SHORT_CONV ** -0.5),
        "filt_w1": nrm(ks[5], (DEPTH, FILT_EMB, FILT_HIDDEN), FILT_EMB ** -0.5),
        "filt_b1": nrm(ks[6], (DEPTH, FILT_HIDDEN), 0.1),
        "filt_freq1": 1.0 + nrm(ks[7], (DEPTH, FILT_HIDDEN), 0.05),
        "filt_w2": nrm(ks[8], (DEPTH, FILT_HIDDEN, FILT_HIDDEN), FILT_HIDDEN ** -0.5),
        "filt_b2": nrm(ks[9], (DEPTH, FILT_HIDDEN), 0.1),
        "filt_freq2": 1.0 + nrm(ks[10], (DEPTH, FILT_HIDDEN), 0.05),
        "filt_w3": nrm(ks[12], (DEPTH, FILT_HIDDEN, N_DIR * H_ORDER * H_WIDTH),
                       FILT_HIDDEN ** -0.5),
        "filt_decay": decay,
        "hyena_d": nrm(ks[13], (DEPTH, H_ORDER, H_WIDTH), 0.1),
        "w_fourier_out": nrm(ks[14], (DEPTH, F_WIDTH, D_MODEL), F_WIDTH ** -0.5),
        "w_hyena_out": nrm(ks[15], (DEPTH, H_WIDTH, D_MODEL), H_WIDTH ** -0.5),
        "b_merge": nrm(ks[16], (DEPTH, N_BRANCH * D_MODEL), 0.01),
        "w_out": nrm(ks[17], (DEPTH, D_MODEL, D_MODEL), D_MODEL ** -0.5),
        "g_post": 1.0 + nrm(ks[18], (DEPTH, D_MODEL), 0.02),
    }


def reference(x_prompt, x_sample, g_pre, w_in, w_short, filt_w1, filt_b1,
              filt_freq1, filt_w2, filt_b2, filt_freq2, filt_w3, filt_decay,
              hyena_d, w_fourier_out, w_hyena_out, b_merge, w_out, g_post):
    y_prompt = x_prompt
    y_sample = x_sample
    for i in range(DEPTH):
        layer_params = (g_pre[i], w_in[i], w_short[i], filt_w1[i], filt_b1[i],
                        filt_freq1[i], filt_w2[i], filt_b2[i], filt_freq2[i],
                        filt_w3[i], filt_decay[i], hyena_d[i], w_fourier_out[i],
                        w_hyena_out[i], b_merge[i], w_out[i], g_post[i])
        y_prompt = encoder_layer(y_prompt, *layer_params)
        y_sample = encoder_layer(y_sample, *layer_params)
    return (y_prompt, y_sample)
```

```python
import functools
import math

import jax
import jax.numpy as jnp
from jax import lax
from jax.experimental import pallas as pl
from jax.experimental.pallas import tpu as pltpu

BF16 = jnp.bfloat16
F32 = jnp.float32

F_GROUPS = 4
RMS_EPS = 1e-6
FILT_BANDS = 16
HALO = 16
V7X_VMEM_BYTES = 64 * 1024 * 1024
VMEM_CAP = V7X_VMEM_BYTES - 8 * 1024 * 1024
HIGHEST = lax.Precision.HIGHEST


def _dot(a, b):
    return jnp.dot(a, b, preferred_element_type=F32)


def _params(block_bytes, n_parallel, n_arbitrary=0):
    limit = min(VMEM_CAP, 2 * block_bytes + 24 * 1024 * 1024)
    sem = ("parallel",) * n_parallel + ("arbitrary",) * n_arbitrary
    return pltpu.CompilerParams(dimension_semantics=sem, vmem_limit_bytes=int(limit))


def _nbytes(shape, dtype):
    return math.prod(shape) * jnp.dtype(dtype).itemsize


def _cis(num, den):
    ang = (num % den).astype(F32) * (2.0 * math.pi / den)
    return jnp.cos(ang), -jnp.sin(ang)


def _block(rr, ri):
    top = jnp.concatenate([rr, -ri], axis=-1)
    bot = jnp.concatenate([ri, rr], axis=-1)
    return jnp.concatenate([top, bot], axis=-2)


def _tables(R):
    N = R * R
    Lh = N // 2
    H = R // 2
    i32 = jnp.int32
    n2 = jnp.arange(R, dtype=i32)[:, None, None]
    k1 = jnp.arange(R, dtype=i32)[None, :, None]
    n1 = jnp.arange(R, dtype=i32)[None, None, :]
    gr, gi = _cis(k1 * (R * n1 + n2), N)
    grh, gih = gr[:, :, :H], gi[:, :, :H]
    f1 = _block(grh, gih)
    grt, git = jnp.swapaxes(grh, 1, 2), jnp.swapaxes(gih, 1, 2)
    i1 = _block(grt, -git) * (1.0 / N)
    fk = jnp.concatenate([gr, gi], axis=1)
    a = jnp.arange(R, dtype=i32)
    wr, wi = _cis(a[:, None] * a[None, :], R)
    m3 = _block(wr, wi)
    m3i = _block(wr, -wi)
    m3f = jnp.concatenate([wr, -wi], axis=1)
    k1f = jnp.arange(H, dtype=i32)[None, :, None]
    n1f = jnp.arange(H, dtype=i32)[None, None, :]
    fr, fi = _cis(k1f * (R * n1f + n2), Lh)
    ff1 = _block(fr, fi)
    cast = lambda t: t.astype(BF16)
    return dict(f1=cast(f1), i1=cast(i1), fk=cast(fk), m3=cast(m3), m3i=cast(m3i),
                m3f=cast(m3f), ff1=cast(ff1))


def _channel_dft(gd):
    a = jnp.arange(gd, dtype=jnp.int32)
    cr, ci = _cis(a[:, None] * a[None, :], gd)
    return jnp.concatenate([cr, ci], axis=1).astype(BF16)


def _inproj_kernel(x_ref, xp_ref, xn_ref, gpre_ref, wf_ref, wh_ref, wm_ref, wsh_ref,
                   bm_ref, cd_ref, z_ref, sfg_ref, v_ref, x1_ref, x2_ref, shg_ref,
                   gf_ref, gh_ref, pext_ref, *, tm, C, D):
    i = pl.program_id(1)
    last = pl.num_programs(1) - 1
    gain = gpre_ref[...]

    def norm(xx):
        ms = jnp.mean(xx * xx, axis=-1, keepdims=True)
        return xx * lax.rsqrt(ms + RMS_EPS) * gain

    xm = norm(x_ref[0]).astype(BF16)
    xprev = (norm(xp_ref[0]) * (i > 0).astype(F32)).astype(BF16)
    xnext = (norm(xn_ref[0]) * (i < last).astype(F32)).astype(BF16)
    xe = jnp.concatenate([xprev, xm, xnext], axis=0)

    pf = _dot(xm, wf_ref[...])
    fg = pf[:, C:]
    sfg_ref[0] = (fg * jax.nn.sigmoid(fg)).astype(BF16)
    fv = pf[:, :C].astype(BF16)
    gd = C // F_GROUPS
    for gi in range(F_GROUPS):
        zz = _dot(fv[:, gi * gd:(gi + 1) * gd], cd_ref[...])
        z_ref[0, 0, :, gi * gd:(gi + 1) * gd] = zz[:, :gd].astype(BF16)
        z_ref[0, 1, :, gi * gd:(gi + 1) * gd] = zz[:, gd:].astype(BF16)

    for idx, o_ref in enumerate((v_ref, x1_ref, x2_ref)):
        pext_ref[...] = _dot(xe, wh_ref[:, idx * C:(idx + 1) * C])
        w = wsh_ref[:, idx * C:(idx + 1) * C]
        conv = (pext_ref[pl.ds(HALO - 1, tm), :] * w[0:1]
                + pext_ref[pl.ds(HALO, tm), :] * w[1:2]
                + pext_ref[pl.ds(HALO + 1, tm), :] * w[2:3])
        o_ref[0] = conv.astype(BF16)

    hg = _dot(xm, wh_ref[:, 3 * C:4 * C])
    shg_ref[0] = (hg * jax.nn.sigmoid(hg)).astype(BF16)
    gate = jax.nn.sigmoid(_dot(xm, wm_ref[...]) + bm_ref[...])
    gf_ref[0] = gate[:, :D].astype(BF16)
    gh_ref[0] = gate[:, D:].astype(BF16)


def _inproj(x, g_pre, wf, wh, wm, w_short, b_merge, cd, *, tm):
    B, L, D = x.shape
    C = D
    nh = tm // HALO
    const = lambda b, i: (0, 0)
    single = dict(pipeline_mode=pl.Buffered(1))
    in_specs = [
        pl.BlockSpec((1, tm, D), lambda b, i: (b, i, 0)),
        pl.BlockSpec((1, HALO, D), lambda b, i: (b, jnp.maximum(i * nh - 1, 0), 0)),
        pl.BlockSpec((1, HALO, D), lambda b, i: (b, jnp.minimum((i + 1) * nh, L // HALO - 1), 0)),
        pl.BlockSpec((1, D), const),
        pl.BlockSpec(wf.shape, const, **single),
        pl.BlockSpec(wh.shape, const, **single),
        pl.BlockSpec(wm.shape, const, **single),
        pl.BlockSpec(w_short.shape, const),
        pl.BlockSpec((1, 2 * D), const),
        pl.BlockSpec(cd.shape, const),
    ]
    row = pl.BlockSpec((1, tm, C), lambda b, i: (b, i, 0))
    out_specs = [pl.BlockSpec((1, 2, tm, C), lambda b, i: (b, 0, i, 0))] + [row] * 7
    act = jax.ShapeDtypeStruct((B, L, C), BF16)
    out_shape = [jax.ShapeDtypeStruct((B, 2, L, C), BF16)] + [act] * 7
    blocks = (_nbytes((tm, D), F32) + 9 * _nbytes((tm, C), BF16)
              + (wf.size + wh.size + wm.size) // 2 * 2 // 2 + 4 * _nbytes((tm, 2 * C), F32))
    return pl.pallas_call(
        functools.partial(_inproj_kernel, tm=tm, C=C, D=D),
        grid=(B, L // tm),
        in_specs=in_specs, out_specs=out_specs, out_shape=out_shape,
        scratch_shapes=[pltpu.VMEM((tm + 2 * HALO, C), F32)],
        compiler_params=_params(blocks, 2),
        name="inproj",
    )(x, x, x, g_pre.reshape(1, D), wf, wh, wm, w_short, b_merge.reshape(1, 2 * D), cd)


def _stage1_kernel(x_ref, tab_ref, o_ref, *, g, W):
    for j in range(g):
        sl = slice(j * W, (j + 1) * W)
        o_ref[0, :, sl] = _dot(tab_ref[j], x_ref[0, :, sl]).astype(BF16)


def _stage1(x, tab, *, g, W, name):
    P, rin, cols = x.shape
    R, rout, _ = tab.shape
    blocks = _nbytes((rin + rout, g * W), BF16) + _nbytes((g, rout, rin), BF16) + _nbytes((rout, W), F32)
    return pl.pallas_call(
        functools.partial(_stage1_kernel, g=g, W=W),
        grid=(R // g, P),
        in_specs=[pl.BlockSpec((1, rin, g * W), lambda t, p: (p, 0, t)),
                  pl.BlockSpec((g, rout, rin), lambda t, p: (t, 0, 0))],
        out_specs=pl.BlockSpec((1, rout, g * W), lambda t, p: (p, 0, t)),
        out_shape=jax.ShapeDtypeStruct((P, rout, cols), BF16),
        compiler_params=_params(blocks, 2),
        name=name,
    )(x, tab)


def _fourier_out_kernel(a_ref, m3f_ref, sfg_ref, wfo_ref, yf_ref, *, kk, R, C, D, scale):
    us = []
    for q in range(kk):
        a = a_ref[0, :, q].reshape(2 * R, C)
        fz = _dot(m3f_ref[...], a)
        u = fz * scale * sfg_ref[0, :, q * C:(q + 1) * C].astype(F32)
        us.append(u.astype(BF16))
    y = _dot(jnp.concatenate(us, axis=0), wfo_ref[...])
    for q in range(kk):
        yf_ref[0, :, q * D:(q + 1) * D] = y[q * R:(q + 1) * R].astype(BF16)


def _fourier_out(a, m3f, sfg, wfo, *, kk, scale):
    B, _, H, R, C = a.shape
    D = wfo.shape[1]
    blocks = (_nbytes((2, kk, R, C), BF16) + _nbytes((R, kk * C), BF16) + _nbytes((R, kk * D), BF16)
              + _nbytes(wfo.shape, BF16) + 2 * _nbytes((kk * R, D), F32))
    return pl.pallas_call(
        functools.partial(_fourier_out_kernel, kk=kk, R=R, C=C, D=D, scale=scale),
        grid=(B, H // kk),
        in_specs=[pl.BlockSpec((1, 2, kk, R, C), lambda b, t: (b, 0, t, 0, 0)),
                  pl.BlockSpec(m3f.shape, lambda b, t: (0, 0)),
                  pl.BlockSpec((1, R, kk * C), lambda b, t: (b, 0, t)),
                  pl.BlockSpec(wfo.shape, lambda b, t: (0, 0))],
        out_specs=pl.BlockSpec((1, R, kk * D), lambda b, t: (b, 0, t)),
        out_shape=jax.ShapeDtypeStruct((B, R, H * D), BF16),
        compiler_params=_params(blocks, 2),
        name="fourier_out",
    )(a, m3f, sfg, wfo)


def _filter_gen_kernel(bands_ref, w1_ref, b1_ref, f1_ref, w2_ref, b2_ref, f2_ref, w3_ref,
                       dec_ref, k_ref, s_ref, *, tn, L):
    i = pl.program_id(0)
    n = i * tn + lax.broadcasted_iota(jnp.int32, (tn, 1), 0)
    t = jnp.where(n < L, n, 2 * L - 1 - n).astype(F32)
    tnorm = t / (L - 1)
    ang = (2.0 * math.pi * t) / L
    lane = lax.broadcasted_iota(jnp.int32, (tn, 128), 1)
    arg = ang * bands_ref[...]
    feats = jnp.where(lane == 0, tnorm,
                      jnp.where(lane <= FILT_BANDS, jnp.cos(arg),
                                jnp.where(lane <= 2 * FILT_BANDS, -jnp.sin(arg), 0.0)))
    hdot = lambda a, b: jnp.dot(a, b, precision=HIGHEST, preferred_element_type=F32)
    h = jnp.sin(f1_ref[...] * (hdot(feats, w1_ref[...]) + b1_ref[...]))
    h = jnp.sin(f2_ref[...] * (hdot(h, w2_ref[...]) + b2_ref[...]))
    h = hdot(h, w3_ref[...]) * jnp.exp(-tnorm * jnp.abs(dec_ref[0]))

    @pl.when(i == 0)
    def _():
        s_ref[...] = jnp.zeros_like(s_ref)

    s_ref[...] += jnp.sum(jnp.abs(h), axis=0, keepdims=True)
    k_ref[...] = jnp.where(n == L, 0.0, h).astype(BF16)


def _filter_gen(L, C, w1, b1, f1, w2, b2, f2, w3, decay, *, tn):
    N = 2 * L
    emb, hid = w1.shape
    bands = jnp.linspace(1e-4, FILT_BANDS - 1, FILT_BANDS, dtype=F32)
    lanes = jnp.zeros((1, 128), F32).at[0, 1:1 + FILT_BANDS].set(bands)
    lanes = lanes.at[0, 1 + FILT_BANDS:1 + 2 * FILT_BANDS].set(bands)
    w1p = jnp.zeros((128, hid), F32).at[:emb].set(w1)
    W = 2 * C
    per_dir = L // tn
    const = lambda i: (0, 0)
    return pl.pallas_call(
        functools.partial(_filter_gen_kernel, tn=tn, L=L),
        grid=(N // tn,),
        in_specs=[pl.BlockSpec((1, 128), const),
                  pl.BlockSpec((128, hid), const), pl.BlockSpec((1, hid), const),
                  pl.BlockSpec((1, hid), const),
                  pl.BlockSpec((hid, hid), const), pl.BlockSpec((1, hid), const),
                  pl.BlockSpec((1, hid), const),
                  pl.BlockSpec((hid, W), lambda i: (0, i // per_dir)),
                  pl.BlockSpec((1, 1, W), lambda i: (i // per_dir, 0, 0))],
        out_specs=[pl.BlockSpec((tn, W), lambda i: (i, 0)),
                   pl.BlockSpec((1, W), const)],
        out_shape=[jax.ShapeDtypeStruct((N, W), BF16), jax.ShapeDtypeStruct((1, W), F32)],
        compiler_params=_params(8 * _nbytes((tn, W), F32), 0, 1),
        name="filter_gen",
    )(lanes, w1p, b1.reshape(1, hid), f1.reshape(1, hid), w2, b2.reshape(1, hid),
      f2.reshape(1, hid), w3, decay.reshape(2, 1, W))


def _filter_spec_kernel(a_ref, m3_ref, s_ref, kf_ref, *, R, C):
    a = a_ref[0, :, 0].reshape(2 * R, 2 * C)
    zf = _dot(m3_ref[...], a) * (1.0 / s_ref[...])
    kf_ref[0, 0] = zf[:, :C].astype(BF16)
    kf_ref[1, 0] = zf[:, C:].astype(BF16)


def _filter_spec(a, m3, s):
    _, _, R, _, W = a.shape
    C = W // 2
    blocks = 2 * _nbytes((2 * R, W), BF16) + 2 * _nbytes((2 * R, W), F32)
    return pl.pallas_call(
        functools.partial(_filter_spec_kernel, R=R, C=C),
        grid=(R,),
        in_specs=[pl.BlockSpec((1, 2, 1, R, W), lambda k: (0, 0, k, 0, 0)),
                  pl.BlockSpec(m3.shape, lambda k: (0, 0)),
                  pl.BlockSpec((1, W), lambda k: (0, 0))],
        out_specs=pl.BlockSpec((2, 1, 2 * R, C), lambda k: (0, k, 0, 0)),
        out_shape=jax.ShapeDtypeStruct((2, R, 2 * R, C), BF16),
        compiler_params=_params(blocks, 1),
        name="filter_spec",
    )(a, m3, s)


def _conv_spec_kernel(a_ref, kf_ref, m3_ref, m3i_ref, b_ref, *, R, C):
    a = a_ref[0, :, 0].reshape(2 * R, C)
    zf = _dot(m3_ref[...], a)
    kf = kf_ref[0, 0].astype(F32)
    zr, zi = zf[:R], zf[R:]
    kr, ki = kf[:R], kf[R:]
    y = jnp.concatenate([zr * kr - zi * ki, zr * ki + zi * kr], axis=0).astype(BF16)
    b_ref[0, :, 0] = _dot(m3i_ref[...], y).astype(BF16).reshape(2, R, C)


def _conv_spec(a, kf, order, m3, m3i):
    P, _, R, _, C = a.shape
    blocks = 3 * _nbytes((2 * R, C), BF16) + 4 * _nbytes((2 * R, C), F32)
    return pl.pallas_call(
        functools.partial(_conv_spec_kernel, R=R, C=C),
        grid=(R, P),
        in_specs=[pl.BlockSpec((1, 2, 1, R, C), lambda k, p: (p, 0, k, 0, 0)),
                  pl.BlockSpec((1, 1, 2 * R, C), lambda k, p: (order, k, 0, 0)),
                  pl.BlockSpec(m3.shape, lambda k, p: (0, 0)),
                  pl.BlockSpec(m3i.shape, lambda k, p: (0, 0))],
        out_specs=pl.BlockSpec((1, 2, 1, R, C), lambda k, p: (p, 0, k, 0, 0)),
        out_shape=jax.ShapeDtypeStruct(a.shape, BF16),
        compiler_params=_params(blocks, 2),
        name="conv_spec",
    )(a, kf, m3, m3i)


def _conv_mid_kernel(b_ref, x1_ref, u_ref, d_ref, i1_ref, f1_ref, z_ref, a_ref, *, g, C):
    for j in range(g):
        sl = slice(j * C, (j + 1) * C)
        y = _dot(i1_ref[j], b_ref[0, :, sl])
        z = x1_ref[0, :, sl].astype(F32) * (y + d_ref[...] * u_ref[0, :, sl].astype(F32))
        zb = z.astype(BF16)
        z_ref[0, :, sl] = zb
        a_ref[0, :, sl] = _dot(f1_ref[j], zb).astype(BF16)


def _conv_mid(b, x1, u, d, i1, f1, *, g):
    P, R2, cols = b.shape
    R = R2 // 2
    C = cols // R
    blocks = (_nbytes((2 * R2 + 3 * R, g * C), BF16) + 2 * _nbytes((g, R2, R), BF16)
              + 4 * _nbytes((R2, C), F32))
    act = pl.BlockSpec((1, R, g * C), lambda t, p: (p, 0, t))
    spec = pl.BlockSpec((1, R2, g * C), lambda t, p: (p, 0, t))
    return pl.pallas_call(
        functools.partial(_conv_mid_kernel, g=g, C=C),
        grid=(R // g, P),
        in_specs=[spec, act, act, pl.BlockSpec((1, C), lambda t, p: (0, 0)),
                  pl.BlockSpec((g, R, R2), lambda t, p: (t, 0, 0)),
                  pl.BlockSpec((g, R2, R), lambda t, p: (t, 0, 0))],
        out_specs=[act, spec],
        out_shape=[jax.ShapeDtypeStruct((P, R, cols), BF16),
                   jax.ShapeDtypeStruct((P, R2, cols), BF16)],
        compiler_params=_params(blocks, 2),
        name="conv_mid",
    )(b, x1, u, d, i1, f1)


def _merge_kernel(b_ref, x2_ref, u_ref, shg_ref, d_ref, i1_ref, who_ref, yf_ref, gf_ref,
                  gh_ref, x_ref, wout_ref, gpost_ref, o_ref, *, g, R, C, D):
    us = []
    for j in range(g):
        sl = slice(j * C, (j + 1) * C)
        y = _dot(i1_ref[j], b_ref[0, :, sl])
        z2 = x2_ref[0, :, sl].astype(F32) * (y + d_ref[...] * u_ref[0, :, sl].astype(F32))
        us.append((z2 * shg_ref[0, :, sl].astype(F32)).astype(BF16))
    yh = _dot(jnp.concatenate(us, axis=0), who_ref[...])
    ms = []
    for j in range(g):
        sl = slice(j * D, (j + 1) * D)
        m = (gf_ref[0, :, sl].astype(F32) * yf_ref[0, :, sl].astype(F32)
             + gh_ref[0, :, sl].astype(F32) * yh[j * R:(j + 1) * R])
        ms.append(m.astype(BF16))
    out = _dot(jnp.concatenate(ms, axis=0), wout_ref[...])
    for j in range(g):
        sl = slice(j * D, (j + 1) * D)
        o = out[j * R:(j + 1) * R]
        var = jnp.mean(o * o, axis=-1, keepdims=True)
        o_ref[0, :, sl] = x_ref[0, :, sl] + o * lax.rsqrt(var + RMS_EPS) * gpost_ref[...]


def _merge(b, x2, u, shg, d, i1, who, yf, gf, gh, x, wout, g_post, *, g):
    P, R2, cols = b.shape
    R = R2 // 2
    C = cols // R
    D = wout.shape[1]
    blocks = (_nbytes((R2 + 3 * R, g * C), BF16) + _nbytes((3 * R, g * D), BF16)
              + 2 * _nbytes((R, g * D), F32) + _nbytes((g, R, R2), BF16)
              + _nbytes(who.shape, BF16) + _nbytes(wout.shape, BF16) + 4 * _nbytes((g * R, D), F32))
    act = pl.BlockSpec((1, R, g * C), lambda t, p: (p, 0, t))
    actd = pl.BlockSpec((1, R, g * D), lambda t, p: (p, 0, t))
    const = lambda t, p: (0, 0)
    return pl.pallas_call(
        functools.partial(_merge_kernel, g=g, R=R, C=C, D=D),
        grid=(R // g, P),
        in_specs=[pl.BlockSpec((1, R2, g * C), lambda t, p: (p, 0, t)), act, act, act,
                  pl.BlockSpec((1, C), const),
                  pl.BlockSpec((g, R, R2), lambda t, p: (t, 0, 0)),
                  pl.BlockSpec(who.shape, const),
                  actd, actd, actd, actd,
                  pl.BlockSpec(wout.shape, const),
                  pl.BlockSpec((1, D), const)],
        out_specs=actd,
        out_shape=jax.ShapeDtypeStruct((P, R, R * D), F32),
        compiler_params=_params(blocks, 2),
        name="merge",
    )(b, x2, u, shg, d, i1, who, yf, gf, gh, x, wout, g_post)


def _tile(n, want):
    t = min(n, want)
    while n % t:
        t //= 2
    return t


def _hyena_filters(L, C, R, tabs, w1, b1, f1, w2, b2, f2, w3, decay):
    k, s = _filter_gen(L, C, w1, b1, f1, w2, b2, f2, w3, decay, tn=_tile(L, 512))
    kp = k.reshape(1, R, R * 2 * C)
    a = _stage1(kp, tabs["fk"], g=_tile(R, 2), W=2 * C, name="filter_stage1")
    return _filter_spec(a.reshape(1, 2, R, R, 2 * C), tabs["m3"], s)


def _encoder_layer(x, kf, tabs, cd, g_pre, wf, wh, wm, w_short, b_merge, hyena_d,
                   wfo, who, wout, g_post, R):
    B, L, D = x.shape
    C = D
    H = R // 2
    P = B // 2
    z, sfg, v, x1, x2, shg, gf, gh = _inproj(x, g_pre, wf, wh, wm, w_short, b_merge, cd,
                                             tm=_tile(L, 512))
    g = _tile(R, 4)

    af = _stage1(z.reshape(B, R, R * C), tabs["ff1"], g=g, W=C, name="fourier_stage1")
    scale = 1.0 / math.sqrt(L * (C // F_GROUPS))
    yf = _fourier_out(af.reshape(B, 2, H, R, C), tabs["m3f"], sfg.reshape(B, R, H * C), wfo,
                      kk=_tile(H, 4), scale=scale)

    pair = lambda t: t.reshape(P, R, R * t.shape[-1])
    d0 = hyena_d[0].reshape(1, C)
    d1 = hyena_d[1].reshape(1, C)
    vp = pair(v)
    a1 = _stage1(vp, tabs["f1"], g=g, W=C, name="conv_stage1")
    b1 = _conv_spec(a1.reshape(P, 2, R, R, C), kf, 0, tabs["m3"], tabs["m3i"])
    z1, a2 = _conv_mid(b1.reshape(P, 2 * R, R * C), pair(x1), vp, d0, tabs["i1"], tabs["f1"], g=g)
    b2 = _conv_spec(a2.reshape(P, 2, R, R, C), kf, 1, tabs["m3"], tabs["m3i"])
    out = _merge(b2.reshape(P, 2 * R, R * C), pair(x2), z1, pair(shg), d1, tabs["i1"], who,
                 pair(yf.reshape(B, L, D)), pair(gf), pair(gh), pair(x), wout,
                 g_post.reshape(1, D), g=g)
    return out.reshape(B, L, D)


def kernel(x_prompt, x_sample, g_pre, w_in, w_short, filt_w1, filt_b1, filt_freq1, filt_w2,
           filt_b2, filt_freq2, filt_w3, filt_decay, hyena_d, w_fourier_out, w_hyena_out,
           b_merge, w_out, g_post):
    depth = g_pre.shape[0]
    L, D = x_prompt.shape[1], x_prompt.shape[2]
    assert x_sample.shape[1:] == (L, D)
    C = D
    R = math.isqrt(2 * L)
    assert R * R == 2 * L and R % 32 == 0
    assert x_prompt.shape[0] % 2 == 0 and x_sample.shape[0] % 2 == 0
    tabs = _tables(R)
    cd = _channel_dft(C // F_GROUPS)
    ys = [x_prompt, x_sample]
    for i in range(depth):
        w = w_in[i].astype(BF16)
        wf, wh, wm = w[:, :2 * C], w[:, 2 * C:6 * C], w[:, 6 * C:]
        kf = _hyena_filters(L, C, R, tabs, filt_w1[i], filt_b1[i], filt_freq1[i], filt_w2[i],
                            filt_b2[i], filt_freq2[i], filt_w3[i], filt_decay[i])
        args = (kf, tabs, cd, g_pre[i], wf, wh, wm, w_short[i], b_merge[i], hyena_d[i],
                w_fourier_out[i].astype(BF16), w_hyena_out[i].astype(BF16),
                w_out[i].astype(BF16), g_post[i], R)
        ys = [_encoder_layer(y, *args) for y in ys]
    return tuple(ys)
```

```python
import functools
import math

import jax
import jax.numpy as jnp
from jax import lax
from jax.experimental import pallas as pl
from jax.experimental.pallas import tpu as pltpu

BF16 = jnp.bfloat16
F32 = jnp.float32

F_GROUPS = 4
RMS_EPS = 1e-6
FILT_BANDS = 16
HALO = 16
LANES = 128
SUB = 8
GROUP = 8
V7X_VMEM_BYTES = 64 * 1024 * 1024
VMEM_CAP = V7X_VMEM_BYTES - 8 * 1024 * 1024
HIGHEST = lax.Precision.HIGHEST


def _dot(a, b):
    return jnp.dot(a, b, preferred_element_type=F32)


def _dot_nt(a, b):
    return lax.dot_general(a, b, (((1,), (1,)), ((), ())), preferred_element_type=F32)


def _params(vmem_bytes, semantics):
    limit = min(VMEM_CAP, vmem_bytes + 8 * 1024 * 1024)
    return pltpu.CompilerParams(dimension_semantics=semantics, vmem_limit_bytes=int(limit))


def _nbytes(shape, dtype):
    return math.prod(shape) * jnp.dtype(dtype).itemsize


def _silu(x):
    return x * jax.nn.sigmoid(x)


def _cis(num, den):
    ang = (num % den).astype(F32) * (2.0 * math.pi / den)
    return jnp.cos(ang), -jnp.sin(ang)


def _block(rr, ri):
    top = jnp.concatenate([rr, -ri], axis=-1)
    bot = jnp.concatenate([ri, rr], axis=-1)
    return jnp.concatenate([top, bot], axis=-2)


def _tables(R):
    N = R * R
    Lh = N // 2
    H = R // 2
    i32 = jnp.int32
    a = jnp.arange(R, dtype=i32)
    wr, wi = _cis(a[:, None] * a[None, :], R)
    f1 = _block(wr[:, :H], wi[:, :H])
    i1 = _block(wr[:, :H].T, -wi[:, :H].T) * (1.0 / N)
    fk = jnp.concatenate([wr, wi], axis=0)
    mb = _block(wr, -wi)
    mbi = _block(wr, wi)
    tr, ti = _cis(a[:, None] * a[None, :], N)
    tw = jnp.stack([jnp.tile(tr, (1, GROUP)), jnp.tile(ti, (1, GROUP))])
    n2 = a[:, None, None]
    k1f = jnp.arange(H, dtype=i32)[None, :, None]
    n1f = jnp.arange(H, dtype=i32)[None, None, :]
    fr, fi = _cis(k1f * (R * n1f + n2), Lh)
    ff1 = _block(fr, fi)
    m3f = jnp.concatenate([wr, -wi], axis=1)
    cast = lambda t: t.astype(BF16)
    return dict(f1=cast(f1), i1=cast(i1), fk=cast(fk), mb=cast(mb), mbi=cast(mbi), tw=tw,
                ff1=cast(ff1), m3f=cast(m3f))


def _channel_dft(gd):
    a = jnp.arange(gd, dtype=jnp.int32)
    cr, ci = _cis(a[:, None] * a[None, :], gd)
    return jnp.concatenate([cr, ci], axis=1).astype(BF16)


def _inproj_nat_kernel(x_ref, gpre_ref, wf_ref, wm_ref, bm_ref, cd_ref,
                       z_ref, sfg_ref, gf_ref, gh_ref, *, C, D):
    xx = x_ref[0]
    ms = jnp.mean(xx * xx, axis=-1, keepdims=True)
    xm = (xx * lax.rsqrt(ms + RMS_EPS) * gpre_ref[...]).astype(BF16)
    pf = _dot(xm, wf_ref[...])
    sfg_ref[0] = _silu(pf[:, C:]).astype(BF16)
    fv = pf[:, :C].astype(BF16)
    gd = C // F_GROUPS
    for gi in range(F_GROUPS):
        zz = _dot(fv[:, gi * gd:(gi + 1) * gd], cd_ref[...])
        z_ref[0, 0, :, gi * gd:(gi + 1) * gd] = zz[:, :gd].astype(BF16)
        z_ref[0, 1, :, gi * gd:(gi + 1) * gd] = zz[:, gd:].astype(BF16)
    gate = jax.nn.sigmoid(_dot(xm, wm_ref[...]) + bm_ref[...])
    gf_ref[0] = gate[:, :D].astype(BF16)
    gh_ref[0] = gate[:, D:].astype(BF16)


def _inproj_nat(x, g_pre, wf, wm, b_merge, cd, *, tm):
    B, L, D = x.shape
    C = D
    const = lambda b, i: (0, 0)
    single = dict(pipeline_mode=pl.Buffered(1))
    row = pl.BlockSpec((1, tm, C), lambda b, i: (b, i, 0))
    act = jax.ShapeDtypeStruct((B, L, C), BF16)
    vmem = (2 * _nbytes((tm, D), F32) + 2 * 5 * _nbytes((tm, C), BF16)
            + _nbytes(wf.shape, BF16) + _nbytes(wm.shape, BF16) + 4 * _nbytes((tm, 2 * C), F32))
    return pl.pallas_call(
        functools.partial(_inproj_nat_kernel, C=C, D=D),
        grid=(B, L // tm),
        in_specs=[pl.BlockSpec((1, tm, D), lambda b, i: (b, i, 0)),
                  pl.BlockSpec((1, D), const),
                  pl.BlockSpec(wf.shape, const, **single),
                  pl.BlockSpec(wm.shape, const, **single),
                  pl.BlockSpec((1, 2 * D), const),
                  pl.BlockSpec(cd.shape, const)],
        out_specs=[pl.BlockSpec((1, 2, tm, C), lambda b, i: (b, 0, i, 0)), row, row, row],
        out_shape=[jax.ShapeDtypeStruct((B, 2, L, C), BF16), act, act, act],
        compiler_params=_params(vmem, ("parallel", "parallel")),
        name="inproj_nat",
    )(x, g_pre.reshape(1, D), wf, wm, b_merge.reshape(1, 2 * D), cd)


def _inproj_cm_kernel(x_ref, xp_ref, xn_ref, gpre_ref, w_ref, wsh_ref, o_ref, xs_ref, xh_ref,
                      *, tp, tch, n_conv):
    i = pl.program_id(1)
    ch = pl.program_id(2)
    last = pl.num_programs(1) - 1
    gain = gpre_ref[...]

    def norm(xx):
        ms = jnp.mean(xx * xx, axis=-1, keepdims=True)
        return xx * lax.rsqrt(ms + RMS_EPS) * gain

    @pl.when(ch == 0)
    def _():
        for s in range(2):
            xs_ref[s] = norm(x_ref[0, s]).astype(BF16)
            xh_ref[s, :HALO] = (norm(xp_ref[0, s]) * (i > 0).astype(F32)).astype(BF16)
            xh_ref[s, HALO:] = (norm(xn_ref[0, s]) * (i < last).astype(F32)).astype(BF16)

    w = w_ref[...]
    taps = wsh_ref[...]
    lane = lax.broadcasted_iota(jnp.int32, (tch, tp), 1)
    is_gate = ch >= n_conv
    for s in range(2):
        p = _dot_nt(w, xs_ref[s])
        ph = _dot_nt(w, xh_ref[s])
        left = jnp.where(lane == 0, ph[:, HALO - 1:HALO], pltpu.roll(p, 1, axis=1))
        right = jnp.where(lane == tp - 1, ph[:, HALO:HALO + 1], pltpu.roll(p, tp - 1, axis=1))
        y = taps[:, 0:1] * left + taps[:, 1:2] * p + taps[:, 2:3] * right
        y = jnp.where(is_gate, _silu(y), y)
        for q in range(tp // LANES):
            o_ref[0, 0, s, 0, pl.ds(q, tch, stride=SUB), :] = y[:, q * LANES:(q + 1) * LANES]


def _inproj_cm(x, g_pre, wht, wsht, *, tch):
    B, L, D = x.shape
    C = D
    P = B // 2
    tp = SUB * LANES
    NB = L // tp
    n_out = wht.shape[0] // C
    per = C // tch
    nh = tp // HALO
    xv = x.reshape(P, 2, L, D)
    vmem = (2 * _nbytes((2, tp, D), F32) + _nbytes((2, tp + 2 * HALO, D), BF16)
            + 2 * _nbytes((tch, D), BF16) + 2 * _nbytes((2, tch, tp), F32)
            + 8 * _nbytes((tch, tp), F32))
    return pl.pallas_call(
        functools.partial(_inproj_cm_kernel, tp=tp, tch=tch, n_conv=(n_out - 1) * per),
        grid=(P, NB, n_out * per),
        in_specs=[pl.BlockSpec((1, 2, tp, D), lambda p, i, c: (p, 0, i, 0)),
                  pl.BlockSpec((1, 2, HALO, D), lambda p, i, c: (p, 0, jnp.maximum(i * nh - 1, 0), 0)),
                  pl.BlockSpec((1, 2, HALO, D),
                               lambda p, i, c: (p, 0, jnp.minimum((i + 1) * nh, L // HALO - 1), 0)),
                  pl.BlockSpec((1, D), lambda p, i, c: (0, 0)),
                  pl.BlockSpec((tch, D), lambda p, i, c: (c, 0)),
                  pl.BlockSpec((tch, 3), lambda p, i, c: (c, 0))],
        out_specs=pl.BlockSpec((1, 1, 2, 1, tch * SUB, LANES),
                               lambda p, i, c: (c // per, p, 0, i, c % per, 0)),
        out_shape=jax.ShapeDtypeStruct((n_out, P, 2, NB, C * SUB, LANES), F32),
        scratch_shapes=[pltpu.VMEM((2, tp, D), BF16), pltpu.VMEM((2, 2 * HALO, D), BF16)],
        compiler_params=_params(vmem, ("parallel", "parallel", "arbitrary")),
        name="inproj_cm",
    )(xv, xv, xv, g_pre.reshape(1, D), wht, wsht)


def _stage1_kernel(x_ref, tab_ref, o_ref, *, g, W):
    for j in range(g):
        sl = slice(j * W, (j + 1) * W)
        o_ref[0, :, sl] = _dot(tab_ref[j], x_ref[0, :, sl]).astype(BF16)


def _stage1(x, tab, *, g, W, name):
    P, rin, cols = x.shape
    R, rout, _ = tab.shape
    vmem = 2 * (_nbytes((rin + rout, g * W), BF16) + _nbytes((g, rout, rin), BF16)) + 2 * _nbytes((rout, W), F32)
    return pl.pallas_call(
        functools.partial(_stage1_kernel, g=g, W=W),
        grid=(R // g, P),
        in_specs=[pl.BlockSpec((1, rin, g * W), lambda t, p: (p, 0, t)),
                  pl.BlockSpec((g, rout, rin), lambda t, p: (t, 0, 0))],
        out_specs=pl.BlockSpec((1, rout, g * W), lambda t, p: (p, 0, t)),
        out_shape=jax.ShapeDtypeStruct((P, rout, cols), BF16),
        compiler_params=_params(vmem, ("parallel", "parallel")),
        name=name,
    )(x, tab)


def _fourier_out_kernel(a_ref, m3f_ref, sfg_ref, wfo_ref, yf_ref, *, kk, R, C, D, scale):
    us = []
    for q in range(kk):
        a = a_ref[0, :, q].reshape(2 * R, C)
        fz = _dot(m3f_ref[...], a)
        u = fz * scale * sfg_ref[0, :, q * C:(q + 1) * C].astype(F32)
        us.append(u.astype(BF16))
    y = _dot(jnp.concatenate(us, axis=0), wfo_ref[...])
    for q in range(kk):
        yf_ref[0, :, q * D:(q + 1) * D] = y[q * R:(q + 1) * R].astype(BF16)


def _fourier_out(a, m3f, sfg, wfo, *, kk, scale):
    B, _, H, R, C = a.shape
    D = wfo.shape[1]
    vmem = (2 * (_nbytes((2, kk, R, C), BF16) + _nbytes((R, kk * C), BF16) + _nbytes((R, kk * D), BF16)
                 + _nbytes(wfo.shape, BF16)) + 3 * _nbytes((kk * R, D), F32))
    return pl.pallas_call(
        functools.partial(_fourier_out_kernel, kk=kk, R=R, C=C, D=D, scale=scale),
        grid=(B, H // kk),
        in_specs=[pl.BlockSpec((1, 2, kk, R, C), lambda b, t: (b, 0, t, 0, 0)),
                  pl.BlockSpec(m3f.shape, lambda b, t: (0, 0)),
                  pl.BlockSpec((1, R, kk * C), lambda b, t: (b, 0, t)),
                  pl.BlockSpec(wfo.shape, lambda b, t: (0, 0))],
        out_specs=pl.BlockSpec((1, R, kk * D), lambda b, t: (b, 0, t)),
        out_shape=jax.ShapeDtypeStruct((B, R, H * D), BF16),
        compiler_params=_params(vmem, ("parallel", "parallel")),
        name="fourier_out",
    )(a, m3f, sfg, wfo)


def _gather_group(load, nb):
    rows = [jnp.concatenate([load(i, j) for j in range(GROUP)], axis=1) for i in range(nb)]
    return jnp.concatenate(rows, axis=0)


def _to_rows(ar, ai):
    R = ar.shape[0]
    parts = [jnp.concatenate([ar[:, j * R:(j + 1) * R], ai[:, j * R:(j + 1) * R]], axis=1)
             for j in range(GROUP)]
    return jnp.concatenate(parts, axis=0)


def _to_lanes(z):
    R = z.shape[1] // 2
    zr = jnp.concatenate([z[j * R:(j + 1) * R, :R] for j in range(GROUP)], axis=1)
    zi = jnp.concatenate([z[j * R:(j + 1) * R, R:] for j in range(GROUP)], axis=1)
    return zr, zi


def _forward(a, tr, ti, mb):
    R = a.shape[0] // 2
    ar, ai = a[:R], a[R:]
    lhs = _to_rows((ar * tr - ai * ti).astype(BF16), (ar * ti + ai * tr).astype(BF16))
    return _dot(lhs, mb)


def _filter_gen_kernel(bands_ref, w1_ref, b1_ref, f1_ref, w2_ref, b2_ref, f2_ref, w3_ref,
                       dec_ref, k_ref, s_ref, *, tp, tch, L):
    i = pl.program_id(0)
    n = i * tp + lax.broadcasted_iota(jnp.int32, (1, tp), 1)
    t = jnp.where(n < L, n, 2 * L - 1 - n).astype(F32)
    tnorm = t / (L - 1)
    ang = (2.0 * math.pi * t) / L
    row = lax.broadcasted_iota(jnp.int32, (LANES, tp), 0)
    arg = bands_ref[...] * ang
    feats = jnp.where(row == 0, tnorm,
                      jnp.where(row <= FILT_BANDS, jnp.cos(arg),
                                jnp.where(row <= 2 * FILT_BANDS, -jnp.sin(arg), 0.0)))
    hdot = lambda a, b: jnp.dot(a, b, precision=HIGHEST, preferred_element_type=F32)
    h = jnp.sin(f1_ref[...] * (hdot(w1_ref[...], feats) + b1_ref[...]))
    h = jnp.sin(f2_ref[...] * (hdot(w2_ref[...], h) + b2_ref[...]))

    @pl.when(i == 0)
    def _():
        s_ref[...] = jnp.zeros_like(s_ref)

    for c in range(w3_ref.shape[0] // tch):
        rows = pl.ds(c * tch, tch)
        hc = hdot(w3_ref[rows, :], h) * jnp.exp(-tnorm * jnp.abs(dec_ref[rows, :]))
        s_ref[rows, :] += jnp.sum(jnp.abs(hc), axis=1, keepdims=True)
        hc = jnp.where(n == L, 0.0, hc)
        for q in range(tp // LANES):
            k_ref[0, pl.ds(c * tch * SUB + q, tch, stride=SUB), :] = hc[:, q * LANES:(q + 1) * LANES]


def _filter_gen(L, C, w1, b1, f1, w2, b2, f2, w3, decay, *, tch):
    N = 2 * L
    tp = SUB * LANES
    emb, hid = w1.shape
    W = 2 * C
    bands = jnp.linspace(1e-4, FILT_BANDS - 1, FILT_BANDS, dtype=F32)
    col = jnp.zeros((LANES, 1), F32).at[1:1 + FILT_BANDS, 0].set(bands)
    col = col.at[1 + FILT_BANDS:1 + 2 * FILT_BANDS, 0].set(bands)
    w1t = jnp.zeros((hid, LANES), F32).at[:, :emb].set(w1.T)
    per_dir = L // tp
    const = lambda i: (0, 0)
    vmem = (2 * _nbytes((W * SUB, LANES), F32) + 2 * _nbytes((W, LANES), F32) * 3
            + 6 * _nbytes((tch, tp), F32) + 8 * _nbytes((LANES, tp), F32))
    return pl.pallas_call(
        functools.partial(_filter_gen_kernel, tp=tp, tch=tch, L=L),
        grid=(N // tp,),
        in_specs=[pl.BlockSpec((LANES, 1), const),
                  pl.BlockSpec((hid, LANES), const), pl.BlockSpec((hid, 1), const),
                  pl.BlockSpec((hid, 1), const),
                  pl.BlockSpec((hid, hid), const), pl.BlockSpec((hid, 1), const),
                  pl.BlockSpec((hid, 1), const),
                  pl.BlockSpec((W, hid), lambda i: (i // per_dir, 0)),
                  pl.BlockSpec((W, 1), lambda i: (i // per_dir, 0))],
        out_specs=[pl.BlockSpec((1, W * SUB, LANES), lambda i: (i, 0, 0)),
                   pl.BlockSpec((W, 1), const)],
        out_shape=[jax.ShapeDtypeStruct((N // tp, W * SUB, LANES), F32),
                   jax.ShapeDtypeStruct((W, 1), F32)],
        compiler_params=_params(vmem, ("arbitrary",)),
        name="filter_gen",
    )(col, w1t, b1.reshape(hid, 1), f1.reshape(hid, 1), w2.T, b2.reshape(hid, 1),
      f2.reshape(hid, 1), w3.T, decay.reshape(2 * W, 1))


def _filter_spec_kernel(k_ref, s_ref, fk_ref, tw_ref, mb_ref, kf_ref, *, tcf, R):
    nb = k_ref.shape[0]

    def body(g, carry):
        r0 = pl.multiple_of(g * GROUP * SUB, GROUP * SUB)
        x = _gather_group(lambda i, j: k_ref[i, pl.ds(r0 + j * SUB, SUB), :], nb)
        a = _dot(fk_ref[...], x.astype(BF16))
        z = _forward(a, tw_ref[0], tw_ref[1], mb_ref[...])
        c0 = pl.program_id(0) * tcf + g * GROUP
        inv = jnp.concatenate(
            [jnp.broadcast_to(1.0 / s_ref[pl.ds(c0 + j, 1), :], (R, 1)) for j in range(GROUP)], axis=0)
        kf_ref[pl.ds(pl.multiple_of(g * GROUP * R, GROUP * R), GROUP * R), :] = z * inv
        return carry

    lax.fori_loop(0, tcf // GROUP, body, 0)


def _filter_spec(k, s, tabs, *, tcf):
    nb, rows, _ = k.shape
    W = rows // SUB
    R = LANES
    vmem = (2 * _nbytes((nb, tcf * SUB, LANES), F32) + 2 * _nbytes((tcf * R, 2 * R), F32)
            + 2 * _nbytes((W, LANES), F32) + 16 * _nbytes((2 * R, GROUP * R), F32))
    return pl.pallas_call(
        functools.partial(_filter_spec_kernel, tcf=tcf, R=R),
        grid=(W // tcf,),
        in_specs=[pl.BlockSpec((nb, tcf * SUB, LANES), lambda c: (0, c, 0)),
                  pl.BlockSpec((W, 1), lambda c: (0, 0)),
                  pl.BlockSpec(tabs["fk"].shape, lambda c: (0, 0)),
                  pl.BlockSpec(tabs["tw"].shape, lambda c: (0, 0, 0)),
                  pl.BlockSpec(tabs["mb"].shape, lambda c: (0, 0))],
        out_specs=pl.BlockSpec((tcf * R, 2 * R), lambda c: (c, 0)),
        out_shape=jax.ShapeDtypeStruct((W * R, 2 * R), F32),
        compiler_params=_params(vmem, ("parallel",)),
        name="filter_spec",
    )(k, s, tabs["fk"], tabs["tw"], tabs["mb"])


def _hyena_kernel(v_ref, x1_ref, x2_ref, hg_ref, kf0_ref, kf1_ref, d_ref, f1_ref, i1_ref,
                  tw_ref, mb_ref, mbi_ref, o_ref, *, tcc, R):
    cb = pl.program_id(0)
    nb = v_ref.shape[3]
    H = R // 2

    def conv(x, kf):
        tr, ti = tw_ref[0], tw_ref[1]
        a = _dot(f1_ref[...], x.astype(BF16))
        z = _forward(a, tr, ti, mb_ref[...])
        zr, zi = z[:, :R], z[:, R:]
        kr, ki = kf[:, :R], kf[:, R:]
        y = jnp.concatenate([zr * kr - zi * ki, zr * ki + zi * kr], axis=1).astype(BF16)
        br, bi = _to_lanes(_dot(y, mbi_ref[...]))
        b = jnp.concatenate([br * tr + bi * ti, bi * tr - br * ti], axis=0).astype(BF16)
        return _dot(i1_ref[...], b)

    def body(g, carry):
        r0 = pl.multiple_of(g * GROUP * SUB, GROUP * SUB)
        k0 = pl.multiple_of(g * GROUP * R, GROUP * R)

        def load(ref):
            halves = [_gather_group(lambda i, j, s=s: ref[0, 0, s, i, pl.ds(r0 + j * SUB, SUB), :], nb)
                      for s in range(2)]
            return jnp.concatenate(halves, axis=0)

        drow = cb * (tcc // GROUP) + g
        v = load(v_ref)
        z1 = load(x1_ref) * (conv(v, kf0_ref[pl.ds(k0, GROUP * R), :]) + d_ref[0, pl.ds(drow, 1), :] * v)
        z2 = load(x2_ref) * (conv(z1, kf1_ref[pl.ds(k0, GROUP * R), :]) + d_ref[1, pl.ds(drow, 1), :] * z1)
        u = z2 * load(hg_ref)
        for s in range(2):
            for i in range(nb):
                for j in range(GROUP):
                    o_ref[0, s, i, pl.ds(r0 + j * SUB, SUB), :] = (
                        u[s * H + i * SUB:s * H + (i + 1) * SUB, j * R:(j + 1) * R])
        return carry

    lax.fori_loop(0, tcc // GROUP, body, 0)


def _hyena(acts, kf, dl, tabs, *, tcc):
    _, P, _, NB, rows, _ = acts.shape
    C = rows // SUB
    R = LANES
    per = C // tcc
    act = lambda w: pl.BlockSpec((1, 1, 2, NB, tcc * SUB, LANES), lambda c, p, w=w: (w, p, 0, 0, c, 0))
    const2 = lambda c, p: (0, 0)
    const3 = lambda c, p: (0, 0, 0)
    vmem = (2 * 5 * _nbytes((2, NB, tcc * SUB, LANES), F32) + 2 * 2 * _nbytes((tcc * R, 2 * R), F32)
            + 2 * _nbytes(dl.shape, F32) + 2 * _nbytes(tabs["tw"].shape, F32)
            + 24 * _nbytes((2 * R, GROUP * R), F32))
    return pl.pallas_call(
        functools.partial(_hyena_kernel, tcc=tcc, R=R),
        grid=(per, P),
        in_specs=[act(0), act(1), act(2), act(3),
                  pl.BlockSpec((tcc * R, 2 * R), lambda c, p: (c, 0)),
                  pl.BlockSpec((tcc * R, 2 * R), lambda c, p: (per + c, 0)),
                  pl.BlockSpec(dl.shape, const3),
                  pl.BlockSpec(tabs["f1"].shape, const2),
                  pl.BlockSpec(tabs["i1"].shape, const2),
                  pl.BlockSpec(tabs["tw"].shape, const3),
                  pl.BlockSpec(tabs["mb"].shape, const2),
                  pl.BlockSpec(tabs["mbi"].shape, const2)],
        out_specs=pl.BlockSpec((1, 2, NB, tcc * SUB, LANES), lambda c, p: (p, 0, 0, c, 0)),
        out_shape=jax.ShapeDtypeStruct((P, 2, NB, rows, LANES), F32),
        compiler_params=_params(vmem, ("parallel", "parallel")),
        name="hyena",
    )(acts, acts, acts, acts, kf, kf, dl, tabs["f1"], tabs["i1"], tabs["tw"], tabs["mb"], tabs["mbi"])


def _merge_kernel(u_ref, yf_ref, gf_ref, gh_ref, x_ref, who_ref, wout_ref, gpost_ref, o_ref, *, C, D):
    q = pl.program_id(2)
    us = [u_ref[0, s, 0, pl.ds(q, C, stride=SUB), :].T.astype(BF16) for s in range(2)]
    yh = _dot(jnp.concatenate(us, axis=0), who_ref[...])
    tp = yh.shape[0] // 2
    m = (gf_ref[0].reshape(2 * tp, D).astype(F32) * yf_ref[0].reshape(2 * tp, D).astype(F32)
         + gh_ref[0].reshape(2 * tp, D).astype(F32) * yh)
    out = _dot(m.astype(BF16), wout_ref[...])
    var = jnp.mean(out * out, axis=-1, keepdims=True)
    res = x_ref[0].reshape(2 * tp, D) + out * lax.rsqrt(var + RMS_EPS) * gpost_ref[...]
    o_ref[0] = res.reshape(2, tp, D)


def _merge(u, yf, gf, gh, x, who, wout, g_post):
    P, _, NB, rows, _ = u.shape
    C = rows // SUB
    B, L, D = x.shape
    pv = lambda t: t.reshape(P, 2, L, D)
    nat = pl.BlockSpec((1, 2, LANES, D), lambda p, i, q: (p, 0, i * SUB + q, 0))
    const = lambda p, i, q: (0, 0)
    vmem = (2 * _nbytes((2, rows, LANES), F32) + 2 * 3 * _nbytes((2, LANES, D), BF16)
            + 4 * _nbytes((2, LANES, D), F32) + 2 * (_nbytes(who.shape, BF16) + _nbytes(wout.shape, BF16))
            + 8 * _nbytes((2 * LANES, D), F32))
    out = pl.pallas_call(
        functools.partial(_merge_kernel, C=C, D=D),
        grid=(P, NB, SUB),
        in_specs=[pl.BlockSpec((1, 2, 1, rows, LANES), lambda p, i, q: (p, 0, i, 0, 0)),
                  nat, nat, nat, nat,
                  pl.BlockSpec(who.shape, const), pl.BlockSpec(wout.shape, const),
                  pl.BlockSpec((1, D), const)],
        out_specs=nat,
        out_shape=jax.ShapeDtypeStruct((P, 2, L, D), F32),
        compiler_params=_params(vmem, ("parallel", "parallel", "arbitrary")),
        name="merge",
    )(u, pv(yf), pv(gf), pv(gh), pv(x), who, wout, g_post.reshape(1, D))
    return out.reshape(B, L, D)


def _tile(n, want):
    t = min(n, want)
    while n % t:
        t //= 2
    return t


def _hyena_filters(L, C, tabs, w1, b1, f1, w2, b2, f2, w3, decay):
    k, s = _filter_gen(L, C, w1, b1, f1, w2, b2, f2, w3, decay, tch=_tile(2 * C, 512))
    return _filter_spec(k, s, tabs, tcf=_tile(2 * C, 32))


def _encoder_layer(x, kf, tabs, cd, dl, g_pre, wf, wht, wsht, wm, b_merge, wfo, who, wout, g_post):
    B, L, D = x.shape
    C = D
    R = LANES
    H = R // 2
    z, sfg, gf, gh = _inproj_nat(x, g_pre, wf, wm, b_merge, cd, tm=_tile(L, 512))
    acts = _inproj_cm(x, g_pre, wht, wsht, tch=_tile(C, 512))

    af = _stage1(z.reshape(B, R, R * C), tabs["ff1"], g=4, W=C, name="fourier_stage1")
    scale = 1.0 / math.sqrt(L * (C // F_GROUPS))
    yf = _fourier_out(af.reshape(B, 2, H, R, C), tabs["m3f"], sfg.reshape(B, R, H * C), wfo,
                      kk=4, scale=scale)

    u = _hyena(acts, kf, dl, tabs, tcc=_tile(C, 32))
    return _merge(u, yf.reshape(B, L, D), gf, gh, x, who, wout, g_post)


def kernel(x_prompt, x_sample, g_pre, w_in, w_short, filt_w1, filt_b1, filt_freq1, filt_w2,
           filt_b2, filt_freq2, filt_w3, filt_decay, hyena_d, w_fourier_out, w_hyena_out,
           b_merge, w_out, g_post):
    depth = g_pre.shape[0]
    L, D = x_prompt.shape[1], x_prompt.shape[2]
    assert x_sample.shape[1:] == (L, D)
    C = D
    R = LANES
    assert R * R == 2 * L and C % (GROUP * F_GROUPS) == 0
    assert x_prompt.shape[0] % 2 == 0 and x_sample.shape[0] % 2 == 0
    tabs = _tables(R)
    cd = _channel_dft(C // F_GROUPS)
    ys = [x_prompt, x_sample]
    for i in range(depth):
        w = w_in[i].astype(BF16)
        wf, wh, wm = w[:, :2 * C], w[:, 2 * C:6 * C], w[:, 6 * C:]
        taps = jnp.concatenate([w_short[i].T, jnp.tile(jnp.array([[0.0, 1.0, 0.0]], F32), (C, 1))], axis=0)
        dl = jnp.repeat(hyena_d[i], R, axis=-1).reshape(2, C // GROUP, GROUP * R)
        kf = _hyena_filters(L, C, tabs, filt_w1[i], filt_b1[i], filt_freq1[i], filt_w2[i],
                            filt_b2[i], filt_freq2[i], filt_w3[i], filt_decay[i])
        args = (kf, tabs, cd, dl, g_pre[i], wf, wh.T, taps, wm, b_merge[i],
                w_fourier_out[i].astype(BF16), w_hyena_out[i].astype(BF16),
                w_out[i].astype(BF16), g_post[i])
        ys = [_encoder_layer(y, *args) for y in ys]
    return tuple(ys)
```

```python
import functools
import math

import jax
import jax.numpy as jnp
from jax import lax
from jax.experimental import pallas as pl
from jax.experimental.pallas import tpu as pltpu

BF16 = jnp.bfloat16
F32 = jnp.float32

F_GROUPS = 4
RMS_EPS = 1e-6
FILT_BANDS = 16
TILE = 16
HALO = TILE
FA = 16
LANES = 128
SUB = 8
GROUP = 8
V7X_VMEM_BYTES = 64 * 1024 * 1024
VMEM_CAP = V7X_VMEM_BYTES - 8 * 1024 * 1024
HIGHEST = lax.Precision.HIGHEST


def _dot(a, b):
    return jnp.dot(a, b, preferred_element_type=F32)


def _dot_nt(a, b):
    return lax.dot_general(a, b, (((1,), (1,)), ((), ())), preferred_element_type=F32)


def _params(vmem_bytes, semantics):
    limit = min(VMEM_CAP, vmem_bytes + 8 * 1024 * 1024)
    return pltpu.CompilerParams(dimension_semantics=semantics, vmem_limit_bytes=int(limit))


def _nbytes(shape, dtype):
    return math.prod(shape) * jnp.dtype(dtype).itemsize


def _silu(x):
    return x * jax.nn.sigmoid(x)


def _cis(num, den):
    ang = (num % den).astype(F32) * (2.0 * math.pi / den)
    return jnp.cos(ang), -jnp.sin(ang)


def _block(rr, ri):
    top = jnp.concatenate([rr, -ri], axis=-1)
    bot = jnp.concatenate([ri, rr], axis=-1)
    return jnp.concatenate([top, bot], axis=-2)


def _tables(R):
    N = R * R
    Lh = N // 2
    H = R // 2
    i32 = jnp.int32
    a = jnp.arange(R, dtype=i32)
    wr, wi = _cis(a[:, None] * a[None, :], R)
    f1 = _block(wr[:, :H], wi[:, :H])
    i1 = _block(wr[:, :H].T, -wi[:, :H].T) * (1.0 / N)
    fk = jnp.concatenate([wr, wi], axis=0)
    mb = _block(wr, -wi)
    mbi = _block(wr, wi)
    tr, ti = _cis(a[:, None] * a[None, :], N)
    tw = jnp.stack([jnp.tile(tr, (1, GROUP)), jnp.tile(ti, (1, GROUP))])
    FB = Lh // FA
    t = jnp.arange(FB // TILE, dtype=i32)[:, None, None, None]
    k1 = jnp.arange(FA, dtype=i32)[None, :, None, None]
    n1 = jnp.arange(FA, dtype=i32)[None, None, :, None]
    j = jnp.arange(TILE, dtype=i32)[None, None, None, :]
    vr, vi = _cis(k1 * (FB * n1 + TILE * t + j), Lh)
    eye = jnp.eye(TILE, dtype=F32)
    expand = lambda v: jnp.einsum("tknj,jm->tkjnm", v, eye).reshape(FB // TILE, FA * TILE, FA * TILE)
    fa = _block(expand(vr), expand(vi))
    b = jnp.arange(FB, dtype=i32)
    br, bi = _cis(b[:, None] * b[None, :], FB)
    fb = jnp.concatenate([br, -bi], axis=1)
    perm = jnp.eye(FA * TILE, dtype=F32).reshape(FA, TILE, FA * TILE).swapaxes(0, 1).reshape(FA * TILE, FA * TILE)
    cast = lambda t: t.astype(BF16)
    return dict(f1=cast(f1), i1=cast(i1), fk=cast(fk), mb=cast(mb), mbi=cast(mbi), tw=tw,
                fa=cast(fa), fb=cast(fb), perm=cast(perm))


def _channel_dft(gd):
    a = jnp.arange(gd, dtype=jnp.int32)
    cr, ci = _cis(a[:, None] * a[None, :], gd)
    return jnp.concatenate([cr, ci], axis=1).astype(BF16)


def _inproj_nat_kernel(x_ref, gpre_ref, wf_ref, wm_ref, bm_ref, cd_ref,
                       z_ref, sfg_ref, gf_ref, gh_ref, *, C, D):
    xx = x_ref[0]
    ms = jnp.mean(xx * xx, axis=-1, keepdims=True)
    xm = (xx * lax.rsqrt(ms + RMS_EPS) * gpre_ref[...]).astype(BF16)
    pf = _dot(xm, wf_ref[...])
    sfg_ref[0] = _silu(pf[:, C:]).astype(BF16)
    fv = pf[:, :C].astype(BF16)
    gd = C // F_GROUPS
    for gi in range(F_GROUPS):
        zz = _dot(fv[:, gi * gd:(gi + 1) * gd], cd_ref[...])
        z_ref[0, 0, :, gi * gd:(gi + 1) * gd] = zz[:, :gd].astype(BF16)
        z_ref[0, 1, :, gi * gd:(gi + 1) * gd] = zz[:, gd:].astype(BF16)
    gate = jax.nn.sigmoid(_dot(xm, wm_ref[...]) + bm_ref[...])
    gf_ref[0] = gate[:, :D].astype(BF16)
    gh_ref[0] = gate[:, D:].astype(BF16)


def _inproj_nat(x, g_pre, wf, wm, b_merge, cd, *, tm):
    B, L, D = x.shape
    C = D
    const = lambda b, i: (0, 0)
    single = dict(pipeline_mode=pl.Buffered(1))
    row = pl.BlockSpec((1, tm, C), lambda b, i: (b, i, 0))
    act = jax.ShapeDtypeStruct((B, L, C), BF16)
    vmem = (2 * _nbytes((tm, D), F32) + 2 * 5 * _nbytes((tm, C), BF16)
            + _nbytes(wf.shape, BF16) + _nbytes(wm.shape, BF16) + 4 * _nbytes((tm, 2 * C), F32))
    return pl.pallas_call(
        functools.partial(_inproj_nat_kernel, C=C, D=D),
        grid=(B, L // tm),
        in_specs=[pl.BlockSpec((1, tm, D), lambda b, i: (b, i, 0)),
                  pl.BlockSpec((1, D), const),
                  pl.BlockSpec(wf.shape, const, **single),
                  pl.BlockSpec(wm.shape, const, **single),
                  pl.BlockSpec((1, 2 * D), const),
                  pl.BlockSpec(cd.shape, const)],
        out_specs=[pl.BlockSpec((1, 2, tm, C), lambda b, i: (b, 0, i, 0)), row, row, row],
        out_shape=[jax.ShapeDtypeStruct((B, 2, L, C), BF16), act, act, act],
        compiler_params=_params(vmem, ("parallel", "parallel")),
        name="inproj_nat",
    )(x, g_pre.reshape(1, D), wf, wm, b_merge.reshape(1, 2 * D), cd)


def _inproj_cm_kernel(x_ref, xp_ref, xn_ref, gpre_ref, w_ref, wsh_ref, o_ref, xs_ref, xh_ref,
                      *, tp, tch, n_conv):
    i = pl.program_id(1)
    ch = pl.program_id(2)
    last = pl.num_programs(1) - 1
    gain = gpre_ref[...]

    def norm(xx):
        ms = jnp.mean(xx * xx, axis=-1, keepdims=True)
        return xx * lax.rsqrt(ms + RMS_EPS) * gain

    @pl.when(ch == 0)
    def _():
        for s in range(2):
            xs_ref[s] = norm(x_ref[0, s]).astype(BF16)
            xh_ref[s, :HALO] = (norm(xp_ref[0, s]) * (i > 0).astype(F32)).astype(BF16)
            xh_ref[s, HALO:] = (norm(xn_ref[0, s]) * (i < last).astype(F32)).astype(BF16)

    w = w_ref[...]

    def store(s, y):
        for q in range(tp // LANES):
            o_ref[0, 0, s, 0, pl.ds(q, tch, stride=SUB), :] = y[:, q * LANES:(q + 1) * LANES]

    @pl.when(ch < n_conv)
    def _():
        taps = wsh_ref[...]
        lane = lax.broadcasted_iota(jnp.int32, (tch, tp), 1)
        for s in range(2):
            p = _dot_nt(w, xs_ref[s])
            ph = _dot_nt(w, xh_ref[s])
            left = jnp.where(lane == 0, ph[:, HALO - 1:HALO], pltpu.roll(p, 1, axis=1))
            right = jnp.where(lane == tp - 1, ph[:, HALO:HALO + 1], pltpu.roll(p, tp - 1, axis=1))
            store(s, taps[:, 0:1] * left + taps[:, 1:2] * p + taps[:, 2:3] * right)

    @pl.when(ch >= n_conv)
    def _():
        for s in range(2):
            store(s, _silu(_dot_nt(w, xs_ref[s])))


def _inproj_cm(x, g_pre, wht, wsht, *, tch):
    B, L, D = x.shape
    C = D
    P = B // 2
    tp = SUB * LANES
    NB = L // tp
    n_out = wht.shape[0] // C
    per = C // tch
    nh = tp // HALO
    n_conv = wsht.shape[0] // tch
    xv = x.reshape(P, 2, L, D)
    vmem = (2 * _nbytes((2, tp, D), F32) + _nbytes((2, tp + 2 * HALO, D), BF16)
            + 2 * _nbytes((tch, D), BF16) + 2 * _nbytes((2, tch, tp), F32)
            + 8 * _nbytes((tch, tp), F32))
    return pl.pallas_call(
        functools.partial(_inproj_cm_kernel, tp=tp, tch=tch, n_conv=n_conv),
        grid=(P, NB, n_out * per),
        in_specs=[pl.BlockSpec((1, 2, tp, D), lambda p, i, c: (p, 0, i, 0)),
                  pl.BlockSpec((1, 2, HALO, D), lambda p, i, c: (p, 0, jnp.maximum(i * nh - 1, 0), 0)),
                  pl.BlockSpec((1, 2, HALO, D),
                               lambda p, i, c: (p, 0, jnp.minimum((i + 1) * nh, L // HALO - 1), 0)),
                  pl.BlockSpec((1, D), lambda p, i, c: (0, 0)),
                  pl.BlockSpec((tch, D), lambda p, i, c: (c, 0)),
                  pl.BlockSpec((tch, 3), lambda p, i, c: (jnp.minimum(c, n_conv - 1), 0))],
        out_specs=pl.BlockSpec((1, 1, 2, 1, tch * SUB, LANES),
                               lambda p, i, c: (c // per, p, 0, i, c % per, 0)),
        out_shape=jax.ShapeDtypeStruct((n_out, P, 2, NB, C * SUB, LANES), F32),
        scratch_shapes=[pltpu.VMEM((2, tp, D), BF16), pltpu.VMEM((2, 2 * HALO, D), BF16)],
        compiler_params=_params(vmem, ("parallel", "parallel", "arbitrary")),
        name="inproj_cm",
    )(xv, xv, xv, g_pre.reshape(1, D), wht, wsht)


def _fourier_a_kernel(z_ref, fa_ref, a_ref, *, tt):
    for tl in range(tt):
        x = z_ref[0, :, :, tl]
        y = _dot(fa_ref[tl], x.reshape(-1, x.shape[-1]))
        a_ref[0, :, :, tl] = y.astype(BF16).reshape(x.shape)


def _fourier_a(z, fa, *, tt):
    B, _, _, T, _, C = z.shape
    blk = (1, 2, FA, tt, TILE, C)
    vmem = 4 * _nbytes(blk, BF16) + 2 * _nbytes((tt,) + fa.shape[1:], BF16) + 4 * _nbytes((2 * FA * TILE, C), F32)
    return pl.pallas_call(
        functools.partial(_fourier_a_kernel, tt=tt),
        grid=(T // tt, B),
        in_specs=[pl.BlockSpec(blk, lambda t, b: (b, 0, 0, t, 0, 0)),
                  pl.BlockSpec((tt,) + fa.shape[1:], lambda t, b: (t, 0, 0))],
        out_specs=pl.BlockSpec(blk, lambda t, b: (b, 0, 0, t, 0, 0)),
        out_shape=jax.ShapeDtypeStruct(z.shape, BF16),
        compiler_params=_params(vmem, ("parallel", "parallel")),
        name="fourier_a",
    )(z, fa)


def _fourier_b_kernel(a_ref, fb_ref, o_ref, *, scale):
    a = a_ref[0, :, 0]
    fz = _dot(fb_ref[...], a.reshape(-1, a.shape[-1])) * scale
    o_ref[0, 0] = fz.astype(BF16)


def _fourier_b(a, fb, *, scale):
    B, _, _, FB, C = a.shape
    vmem = 2 * 3 * _nbytes((FB, C), BF16) + 2 * _nbytes(fb.shape, BF16) + 3 * _nbytes((FB, C), F32)
    return pl.pallas_call(
        functools.partial(_fourier_b_kernel, scale=scale),
        grid=(B, FA),
        in_specs=[pl.BlockSpec((1, 2, 1, FB, C), lambda b, k: (b, 0, k, 0, 0)),
                  pl.BlockSpec(fb.shape, lambda b, k: (0, 0))],
        out_specs=pl.BlockSpec((1, 1, FB, C), lambda b, k: (b, k, 0, 0)),
        out_shape=jax.ShapeDtypeStruct((B, FA, FB, C), BF16),
        compiler_params=_params(vmem, ("parallel", "parallel")),
        name="fourier_b",
    )(a, fb)


def _gather_group(load, nb):
    rows = [jnp.concatenate([load(i, j) for j in range(GROUP)], axis=1) for i in range(nb)]
    return jnp.concatenate(rows, axis=0)


def _to_rows(ar, ai):
    R = ar.shape[0]
    parts = [jnp.concatenate([ar[:, j * R:(j + 1) * R], ai[:, j * R:(j + 1) * R]], axis=1)
             for j in range(GROUP)]
    return jnp.concatenate(parts, axis=0)


def _to_lanes(z):
    R = z.shape[1] // 2
    zr = jnp.concatenate([z[j * R:(j + 1) * R, :R] for j in range(GROUP)], axis=1)
    zi = jnp.concatenate([z[j * R:(j + 1) * R, R:] for j in range(GROUP)], axis=1)
    return zr, zi


def _forward(a, tr, ti, mb):
    R = a.shape[0] // 2
    ar, ai = a[:R], a[R:]
    lhs = _to_rows((ar * tr - ai * ti).astype(BF16), (ar * ti + ai * tr).astype(BF16))
    return _dot(lhs, mb)


def _filter_gen_kernel(bands_ref, w1_ref, b1_ref, f1_ref, w2_ref, b2_ref, f2_ref, w3_ref,
                       dec_ref, k_ref, s_ref, *, tp, tch, L):
    i = pl.program_id(0)
    n = i * tp + lax.broadcasted_iota(jnp.int32, (1, tp), 1)
    t = jnp.where(n < L, n, 2 * L - 1 - n).astype(F32)
    tnorm = t / (L - 1)
    ang = (2.0 * math.pi * t) / L
    row = lax.broadcasted_iota(jnp.int32, (LANES, tp), 0)
    arg = bands_ref[...] * ang
    feats = jnp.where(row == 0, tnorm,
                      jnp.where(row <= FILT_BANDS, jnp.cos(arg),
                                jnp.where(row <= 2 * FILT_BANDS, -jnp.sin(arg), 0.0)))
    hdot = lambda a, b: jnp.dot(a, b, precision=HIGHEST, preferred_element_type=F32)
    h = jnp.sin(f1_ref[...] * (hdot(w1_ref[...], feats) + b1_ref[...]))
    h = jnp.sin(f2_ref[...] * (hdot(w2_ref[...], h) + b2_ref[...]))

    @pl.when(i == 0)
    def _():
        s_ref[...] = jnp.zeros_like(s_ref)

    for c in range(w3_ref.shape[0] // tch):
        rows = pl.ds(c * tch, tch)
        hc = hdot(w3_ref[rows, :], h) * jnp.exp(-tnorm * jnp.abs(dec_ref[rows, :]))
        s_ref[rows, :] += jnp.sum(jnp.abs(hc), axis=1, keepdims=True)
        hc = jnp.where(n == L, 0.0, hc)
        for q in range(tp // LANES):
            k_ref[0, pl.ds(c * tch * SUB + q, tch, stride=SUB), :] = hc[:, q * LANES:(q + 1) * LANES]


def _filter_gen(L, C, w1, b1, f1, w2, b2, f2, w3, decay, *, tch):
    N = 2 * L
    tp = SUB * LANES
    emb, hid = w1.shape
    W = 2 * C
    bands = jnp.linspace(1e-4, FILT_BANDS - 1, FILT_BANDS, dtype=F32)
    col = jnp.zeros((LANES, 1), F32).at[1:1 + FILT_BANDS, 0].set(bands)
    col = col.at[1 + FILT_BANDS:1 + 2 * FILT_BANDS, 0].set(bands)
    w1t = jnp.zeros((hid, LANES), F32).at[:, :emb].set(w1.T)
    per_dir = L // tp
    const = lambda i: (0, 0)
    vmem = (2 * _nbytes((W * SUB, LANES), F32) + 2 * _nbytes((W, LANES), F32) * 3
            + 6 * _nbytes((tch, tp), F32) + 8 * _nbytes((LANES, tp), F32))
    return pl.pallas_call(
        functools.partial(_filter_gen_kernel, tp=tp, tch=tch, L=L),
        grid=(N // tp,),
        in_specs=[pl.BlockSpec((LANES, 1), const),
                  pl.BlockSpec((hid, LANES), const), pl.BlockSpec((hid, 1), const),
                  pl.BlockSpec((hid, 1), const),
                  pl.BlockSpec((hid, hid), const), pl.BlockSpec((hid, 1), const),
                  pl.BlockSpec((hid, 1), const),
                  pl.BlockSpec((W, hid), lambda i: (i // per_dir, 0)),
                  pl.BlockSpec((W, 1), lambda i: (i // per_dir, 0))],
        out_specs=[pl.BlockSpec((1, W * SUB, LANES), lambda i: (i, 0, 0)),
                   pl.BlockSpec((W, 1), const)],
        out_shape=[jax.ShapeDtypeStruct((N // tp, W * SUB, LANES), F32),
                   jax.ShapeDtypeStruct((W, 1), F32)],
        compiler_params=_params(vmem, ("arbitrary",)),
        name="filter_gen",
    )(col, w1t, b1.reshape(hid, 1), f1.reshape(hid, 1), w2.T, b2.reshape(hid, 1),
      f2.reshape(hid, 1), w3.T, decay.reshape(2 * W, 1))


def _filter_spec_kernel(k_ref, s_ref, fk_ref, tw_ref, mb_ref, kf_ref, *, tcf, R):
    nb = k_ref.shape[0]

    def body(g, carry):
        r0 = pl.multiple_of(g * GROUP * SUB, GROUP * SUB)
        x = _gather_group(lambda i, j: k_ref[i, pl.ds(r0 + j * SUB, SUB), :], nb)
        a = _dot(fk_ref[...], x.astype(BF16))
        z = _forward(a, tw_ref[0], tw_ref[1], mb_ref[...])
        c0 = pl.program_id(0) * tcf + g * GROUP
        inv = jnp.concatenate(
            [jnp.broadcast_to(1.0 / s_ref[pl.ds(c0 + j, 1), :], (R, 1)) for j in range(GROUP)], axis=0)
        kf_ref[pl.ds(pl.multiple_of(g * GROUP * R, GROUP * R), GROUP * R), :] = z * inv
        return carry

    lax.fori_loop(0, tcf // GROUP, body, 0)


def _filter_spec(k, s, tabs, *, tcf):
    nb, rows, _ = k.shape
    W = rows // SUB
    R = LANES
    vmem = (2 * _nbytes((nb, tcf * SUB, LANES), F32) + 2 * _nbytes((tcf * R, 2 * R), F32)
            + 2 * _nbytes((W, LANES), F32) + 16 * _nbytes((2 * R, GROUP * R), F32))
    return pl.pallas_call(
        functools.partial(_filter_spec_kernel, tcf=tcf, R=R),
        grid=(W // tcf,),
        in_specs=[pl.BlockSpec((nb, tcf * SUB, LANES), lambda c: (0, c, 0)),
                  pl.BlockSpec((W, 1), lambda c: (0, 0)),
                  pl.BlockSpec(tabs["fk"].shape, lambda c: (0, 0)),
                  pl.BlockSpec(tabs["tw"].shape, lambda c: (0, 0, 0)),
                  pl.BlockSpec(tabs["mb"].shape, lambda c: (0, 0))],
        out_specs=pl.BlockSpec((tcf * R, 2 * R), lambda c: (c, 0)),
        out_shape=jax.ShapeDtypeStruct((W * R, 2 * R), F32),
        compiler_params=_params(vmem, ("parallel",)),
        name="filter_spec",
    )(k, s, tabs["fk"], tabs["tw"], tabs["mb"])


def _hyena_kernel(v_ref, x1_ref, x2_ref, hg_ref, kf0_ref, kf1_ref, d_ref, f1_ref, i1_ref,
                  tw_ref, mb_ref, mbi_ref, o_ref, *, tcc, R):
    cb = pl.program_id(0)
    nb = v_ref.shape[3]
    H = R // 2

    def conv(x, kf):
        tr, ti = tw_ref[0], tw_ref[1]
        a = _dot(f1_ref[...], x.astype(BF16))
        z = _forward(a, tr, ti, mb_ref[...])
        zr, zi = z[:, :R], z[:, R:]
        kr, ki = kf[:, :R], kf[:, R:]
        y = jnp.concatenate([zr * kr - zi * ki, zr * ki + zi * kr], axis=1).astype(BF16)
        br, bi = _to_lanes(_dot(y, mbi_ref[...]))
        b = jnp.concatenate([br * tr + bi * ti, bi * tr - br * ti], axis=0).astype(BF16)
        return _dot(i1_ref[...], b)

    def body(g, carry):
        r0 = pl.multiple_of(g * GROUP * SUB, GROUP * SUB)
        k0 = pl.multiple_of(g * GROUP * R, GROUP * R)

        def load(ref):
            halves = [_gather_group(lambda i, j, s=s: ref[0, 0, s, i, pl.ds(r0 + j * SUB, SUB), :], nb)
                      for s in range(2)]
            return jnp.concatenate(halves, axis=0)

        drow = cb * (tcc // GROUP) + g
        v = load(v_ref)
        z1 = load(x1_ref) * (conv(v, kf0_ref[pl.ds(k0, GROUP * R), :]) + d_ref[0, pl.ds(drow, 1), :] * v)
        z2 = load(x2_ref) * (conv(z1, kf1_ref[pl.ds(k0, GROUP * R), :]) + d_ref[1, pl.ds(drow, 1), :] * z1)
        u = z2 * load(hg_ref)
        for s in range(2):
            for i in range(nb):
                for j in range(GROUP):
                    o_ref[0, s, i, pl.ds(r0 + j * SUB, SUB), :] = (
                        u[s * H + i * SUB:s * H + (i + 1) * SUB, j * R:(j + 1) * R])
        return carry

    lax.fori_loop(0, tcc // GROUP, body, 0, unroll=2)


def _hyena(acts, kf, dl, tabs, *, tcc):
    _, P, _, NB, rows, _ = acts.shape
    C = rows // SUB
    R = LANES
    per = C // tcc
    act = lambda w: pl.BlockSpec((1, 1, 2, NB, tcc * SUB, LANES), lambda c, p, w=w: (w, p, 0, 0, c, 0))
    const2 = lambda c, p: (0, 0)
    const3 = lambda c, p: (0, 0, 0)
    vmem = (2 * 5 * _nbytes((2, NB, tcc * SUB, LANES), F32) + 2 * 2 * _nbytes((tcc * R, 2 * R), F32)
            + 2 * _nbytes(dl.shape, F32) + 2 * _nbytes(tabs["tw"].shape, F32)
            + 24 * _nbytes((2 * R, GROUP * R), F32))
    return pl.pallas_call(
        functools.partial(_hyena_kernel, tcc=tcc, R=R),
        grid=(per, P),
        in_specs=[act(0), act(1), act(2), act(3),
                  pl.BlockSpec((tcc * R, 2 * R), lambda c, p: (c, 0)),
                  pl.BlockSpec((tcc * R, 2 * R), lambda c, p: (per + c, 0)),
                  pl.BlockSpec(dl.shape, const3),
                  pl.BlockSpec(tabs["f1"].shape, const2),
                  pl.BlockSpec(tabs["i1"].shape, const2),
                  pl.BlockSpec(tabs["tw"].shape, const3),
                  pl.BlockSpec(tabs["mb"].shape, const2),
                  pl.BlockSpec(tabs["mbi"].shape, const2)],
        out_specs=pl.BlockSpec((1, 2, NB, tcc * SUB, LANES), lambda c, p: (p, 0, 0, c, 0)),
        out_shape=jax.ShapeDtypeStruct((P, 2, NB, rows, LANES), F32),
        compiler_params=_params(vmem, ("parallel", "parallel")),
        name="hyena",
    )(acts, acts, acts, acts, kf, kf, dl, tabs["f1"], tabs["i1"], tabs["tw"], tabs["mb"], tabs["mbi"])


def _merge_kernel(u_ref, fz_ref, sfg_ref, gf_ref, gh_ref, x_ref, perm_ref, wfo_ref, who_ref,
                  wout_ref, gpost_ref, o_ref, *, C, D, qq):
    q0 = pl.program_id(2) * qq
    us = [u_ref[0, s, 0, pl.ds(q0 + q, C, stride=SUB), :].T.astype(BF16)
          for s in range(2) for q in range(qq)]
    yh = _dot(jnp.concatenate(us, axis=0), who_ref[...])
    tp = qq * LANES
    nat = lambda ref: ref[0].reshape(2 * tp, ref.shape[-1])
    fz = jnp.concatenate([_dot(perm_ref[...], fz_ref[0, s].reshape(tp, C)) for s in range(2)], axis=0)
    yf = _dot((fz * nat(sfg_ref).astype(F32)).astype(BF16), wfo_ref[...])
    m = nat(gf_ref).astype(F32) * yf + nat(gh_ref).astype(F32) * yh
    out = _dot(m.astype(BF16), wout_ref[...])
    var = jnp.mean(out * out, axis=-1, keepdims=True)
    res = nat(x_ref) + out * lax.rsqrt(var + RMS_EPS) * gpost_ref[...]
    o_ref[0] = res.reshape(2, tp, D)


def _merge(u, fz, sfg, gf, gh, x, perm, wfo, who, wout, g_post):
    P, _, NB, rows, _ = u.shape
    C = rows // SUB
    B, L, D = x.shape
    tp = FA * TILE
    qq = tp // LANES
    pv = lambda t: t.reshape(P, 2, L, t.shape[-1])
    nat = lambda w: pl.BlockSpec((1, 2, tp, w), lambda p, i, q: (p, 0, i * (SUB // qq) + q, 0))
    const = lambda p, i, q: (0, 0)
    single = dict(pipeline_mode=pl.Buffered(1))
    weights = [perm, wfo, who, wout]
    vmem = (2 * _nbytes((2, rows, LANES), F32) + 2 * 4 * _nbytes((2, tp, D), BF16)
            + 4 * _nbytes((2, tp, D), F32) + sum(_nbytes(w.shape, BF16) for w in weights)
            + 8 * _nbytes((2 * tp, D), F32))
    out = pl.pallas_call(
        functools.partial(_merge_kernel, C=C, D=D, qq=qq),
        grid=(P, NB, SUB // qq),
        in_specs=[pl.BlockSpec((1, 2, 1, rows, LANES), lambda p, i, q: (p, 0, i, 0, 0)),
                  pl.BlockSpec((1, 2, FA, TILE, C), lambda p, i, q: (p, 0, 0, i * (SUB // qq) + q, 0)),
                  nat(C), nat(D), nat(D), nat(D)]
                 + [pl.BlockSpec(w.shape, const, **single) for w in weights]
                 + [pl.BlockSpec((1, D), const)],
        out_specs=nat(D),
        out_shape=jax.ShapeDtypeStruct((P, 2, L, D), F32),
        compiler_params=_params(vmem, ("parallel", "parallel", "arbitrary")),
        name="merge",
    )(u, fz.reshape((P, 2) + fz.shape[1:]), pv(sfg), pv(gf), pv(gh), pv(x), *weights,
      g_post.reshape(1, D))
    return out.reshape(B, L, D)


def _tile(n, want):
    t = min(n, want)
    while n % t:
        t //= 2
    return t


def _hyena_filters(L, C, tabs, w1, b1, f1, w2, b2, f2, w3, decay):
    k, s = _filter_gen(L, C, w1, b1, f1, w2, b2, f2, w3, decay, tch=_tile(2 * C, 512))
    return _filter_spec(k, s, tabs, tcf=_tile(2 * C, 32))


def _encoder_layer(x, kf, tabs, cd, dl, g_pre, wf, wht, wsht, wm, b_merge, wfo, who, wout, g_post):
    B, L, D = x.shape
    C = D
    FB = L // FA
    z, sfg, gf, gh = _inproj_nat(x, g_pre, wf, wm, b_merge, cd, tm=_tile(L, 512))
    acts = _inproj_cm(x, g_pre, wht, wsht, tch=_tile(C, 512))

    af = _fourier_a(z.reshape(B, 2, FA, FB // TILE, TILE, C), tabs["fa"], tt=4)
    scale = 1.0 / math.sqrt(L * (C // F_GROUPS))
    fz = _fourier_b(af.reshape(B, 2, FA, FB, C), tabs["fb"], scale=scale)

    u = _hyena(acts, kf, dl, tabs, tcc=_tile(C, 32))
    return _merge(u, fz, sfg, gf, gh, x, tabs["perm"], wfo, who, wout, g_post)


def kernel(x_prompt, x_sample, g_pre, w_in, w_short, filt_w1, filt_b1, filt_freq1, filt_w2,
           filt_b2, filt_freq2, filt_w3, filt_decay, hyena_d, w_fourier_out, w_hyena_out,
           b_merge, w_out, g_post):
    depth = g_pre.shape[0]
    L, D = x_prompt.shape[1], x_prompt.shape[2]
    assert x_sample.shape[1:] == (L, D)
    C = D
    R = LANES
    assert R * R == 2 * L and C % (GROUP * F_GROUPS) == 0
    assert x_prompt.shape[0] % 2 == 0 and x_sample.shape[0] % 2 == 0
    tabs = _tables(R)
    cd = _channel_dft(C // F_GROUPS)
    ys = [x_prompt, x_sample]
    for i in range(depth):
        w = w_in[i].astype(BF16)
        wf, wh, wm = w[:, :2 * C], w[:, 2 * C:6 * C], w[:, 6 * C:]
        taps = w_short[i].T
        dl = jnp.repeat(hyena_d[i], R, axis=-1).reshape(2, C // GROUP, GROUP * R)
        kf = _hyena_filters(L, C, tabs, filt_w1[i], filt_b1[i], filt_freq1[i], filt_w2[i],
                            filt_b2[i], filt_freq2[i], filt_w3[i], filt_decay[i])
        args = (kf, tabs, cd, dl, g_pre[i], wf, wh.T, taps, wm, b_merge[i],
                w_fourier_out[i].astype(BF16), w_hyena_out[i].astype(BF16),
                w_out[i].astype(BF16), g_post[i])
        ys = [_encoder_layer(y, *args) for y in ys]
    return tuple(ys)
```

```python
import functools
import math

import jax
import jax.numpy as jnp
from jax import lax
from jax.experimental import pallas as pl
from jax.experimental.pallas import tpu as pltpu

BF16 = jnp.bfloat16
F32 = jnp.float32

F_GROUPS = 4
RMS_EPS = 1e-6
FILT_BANDS = 16
TILE = 16
HALO = TILE
FA = 16
LANES = 128
SUB = 8
GROUP = 8
V7X_VMEM_BYTES = 64 * 1024 * 1024
VMEM_CAP = V7X_VMEM_BYTES - 8 * 1024 * 1024
HIGHEST = lax.Precision.HIGHEST


def _dot(a, b):
    return jnp.dot(a, b, preferred_element_type=F32)


def _dot_nt(a, b):
    return lax.dot_general(a, b, (((1,), (1,)), ((), ())), preferred_element_type=F32)


def _params(vmem_bytes, semantics):
    limit = min(VMEM_CAP, vmem_bytes + 8 * 1024 * 1024)
    return pltpu.CompilerParams(dimension_semantics=semantics, vmem_limit_bytes=int(limit))


def _nbytes(shape, dtype):
    return math.prod(shape) * jnp.dtype(dtype).itemsize


def _silu(x):
    return x * jax.nn.sigmoid(x)


def _cis(num, den):
    ang = (num % den).astype(F32) * (2.0 * math.pi / den)
    return jnp.cos(ang), -jnp.sin(ang)


def _block(rr, ri):
    top = jnp.concatenate([rr, -ri], axis=-1)
    bot = jnp.concatenate([ri, rr], axis=-1)
    return jnp.concatenate([top, bot], axis=-2)


def _tables(R):
    N = R * R
    Lh = N // 2
    H = R // 2
    i32 = jnp.int32
    a = jnp.arange(R, dtype=i32)
    wr, wi = _cis(a[:, None] * a[None, :], R)
    f1 = _block(wr[:, :H], wi[:, :H])
    i1 = _block(wr[:, :H].T, -wi[:, :H].T) * (1.0 / N)
    fk = jnp.concatenate([wr, wi], axis=0)
    mb = _block(wr, -wi)
    mbi = _block(wr, wi)
    tr, ti = _cis(a[:, None] * a[None, :], N)
    tw = jnp.stack([jnp.tile(tr, (1, GROUP)), jnp.tile(ti, (1, GROUP))])
    FB = Lh // FA
    t = jnp.arange(FB // TILE, dtype=i32)[:, None, None, None]
    k1 = jnp.arange(FA, dtype=i32)[None, :, None, None]
    n1 = jnp.arange(FA, dtype=i32)[None, None, :, None]
    j = jnp.arange(TILE, dtype=i32)[None, None, None, :]
    vr, vi = _cis(k1 * (FB * n1 + TILE * t + j), Lh)
    eye = jnp.eye(TILE, dtype=F32)
    expand = lambda v: jnp.einsum("tknj,jm->tkjnm", v, eye).reshape(FB // TILE, FA * TILE, FA * TILE)
    fa = _block(expand(vr), expand(vi))
    b = jnp.arange(FB, dtype=i32)
    br, bi = _cis(b[:, None] * b[None, :], FB)
    fb = jnp.concatenate([br, -bi], axis=1)
    perm = jnp.eye(FA * TILE, dtype=F32).reshape(FA, TILE, FA * TILE).swapaxes(0, 1).reshape(FA * TILE, FA * TILE)
    cast = lambda t: t.astype(BF16)
    return dict(f1=cast(f1), i1=cast(i1), fk=cast(fk), mb=cast(mb), mbi=cast(mbi), tw=cast(tw),
                fa=cast(fa), fb=cast(fb), perm=cast(perm))


def _channel_dft(gd):
    a = jnp.arange(gd, dtype=jnp.int32)
    cr, ci = _cis(a[:, None] * a[None, :], gd)
    return jnp.concatenate([cr, ci], axis=1).astype(BF16)


def _inproj_nat_kernel(x_ref, gpre_ref, wf_ref, wm_ref, bm_ref, cd_ref,
                       z_ref, sfg_ref, gf_ref, gh_ref, *, C, D):
    xx = x_ref[0]
    ms = jnp.mean(xx * xx, axis=-1, keepdims=True)
    xm = (xx * lax.rsqrt(ms + RMS_EPS) * gpre_ref[...]).astype(BF16)
    pf = _dot(xm, wf_ref[...])
    sfg_ref[0] = _silu(pf[:, C:]).astype(BF16)
    fv = pf[:, :C].astype(BF16)
    gd = C // F_GROUPS
    for gi in range(F_GROUPS):
        zz = _dot(fv[:, gi * gd:(gi + 1) * gd], cd_ref[...])
        z_ref[0, 0, :, gi * gd:(gi + 1) * gd] = zz[:, :gd].astype(BF16)
        z_ref[0, 1, :, gi * gd:(gi + 1) * gd] = zz[:, gd:].astype(BF16)
    gate = jax.nn.sigmoid(_dot(xm, wm_ref[...]) + bm_ref[...])
    gf_ref[0] = gate[:, :D].astype(BF16)
    gh_ref[0] = gate[:, D:].astype(BF16)


def _inproj_nat(x, g_pre, wf, wm, b_merge, cd, *, tm):
    B, L, D = x.shape
    C = D
    const = lambda b, i: (0, 0)
    single = dict(pipeline_mode=pl.Buffered(1))
    row = pl.BlockSpec((1, tm, C), lambda b, i: (b, i, 0))
    act = jax.ShapeDtypeStruct((B, L, C), BF16)
    vmem = (2 * _nbytes((tm, D), F32) + 2 * 5 * _nbytes((tm, C), BF16)
            + _nbytes(wf.shape, BF16) + _nbytes(wm.shape, BF16) + 4 * _nbytes((tm, 2 * C), F32))
    return pl.pallas_call(
        functools.partial(_inproj_nat_kernel, C=C, D=D),
        grid=(B, L // tm),
        in_specs=[pl.BlockSpec((1, tm, D), lambda b, i: (b, i, 0)),
                  pl.BlockSpec((1, D), const),
                  pl.BlockSpec(wf.shape, const, **single),
                  pl.BlockSpec(wm.shape, const, **single),
                  pl.BlockSpec((1, 2 * D), const),
                  pl.BlockSpec(cd.shape, const)],
        out_specs=[pl.BlockSpec((1, 2, tm, C), lambda b, i: (b, 0, i, 0)), row, row, row],
        out_shape=[jax.ShapeDtypeStruct((B, 2, L, C), BF16), act, act, act],
        compiler_params=_params(vmem, ("parallel", "parallel")),
        name="inproj_nat",
    )(x, g_pre.reshape(1, D), wf, wm, b_merge.reshape(1, 2 * D), cd)


def _inproj_cm_kernel(x_ref, xp_ref, xn_ref, gpre_ref, w_ref, wsh_ref, o_ref, xs_ref, xh_ref,
                      *, tp, tch, n_conv):
    i = pl.program_id(1)
    ch = pl.program_id(2)
    last = pl.num_programs(1) - 1
    gain = gpre_ref[...]

    def norm(xx):
        ms = jnp.mean(xx * xx, axis=-1, keepdims=True)
        return xx * lax.rsqrt(ms + RMS_EPS) * gain

    @pl.when(ch == 0)
    def _():
        for s in range(2):
            xs_ref[s] = norm(x_ref[0, s]).T.astype(BF16)
            xh_ref[s, :HALO] = (norm(xp_ref[0, s]) * (i > 0).astype(F32)).astype(BF16)
            xh_ref[s, HALO:] = (norm(xn_ref[0, s]) * (i < last).astype(F32)).astype(BF16)

    w = w_ref[...]

    def store(s, y):
        for q in range(tp // LANES):
            o_ref[0, 0, s, 0, pl.ds(q, tch, stride=SUB), :] = y[:, q * LANES:(q + 1) * LANES]

    @pl.when(ch < n_conv)
    def _():
        taps = wsh_ref[...]
        lane = lax.broadcasted_iota(jnp.int32, (tch, tp), 1)
        for s in range(2):
            p = _dot(w, xs_ref[s])
            ph = _dot_nt(w, xh_ref[s])
            left = jnp.where(lane == 0, ph[:, HALO - 1:HALO], pltpu.roll(p, 1, axis=1))
            right = jnp.where(lane == tp - 1, ph[:, HALO:HALO + 1], pltpu.roll(p, tp - 1, axis=1))
            store(s, taps[:, 0:1] * left + taps[:, 1:2] * p + taps[:, 2:3] * right)

    @pl.when(ch >= n_conv)
    def _():
        for s in range(2):
            store(s, _silu(_dot(w, xs_ref[s])))


def _inproj_cm(x, g_pre, wht, wsht, *, tch):
    B, L, D = x.shape
    C = D
    P = B // 2
    tp = SUB * LANES
    NB = L // tp
    n_out = wht.shape[0] // C
    per = C // tch
    nh = tp // HALO
    n_conv = wsht.shape[0] // tch
    xv = x.reshape(P, 2, L, D)
    vmem = (2 * _nbytes((2, tp, D), F32) + _nbytes((2, tp + 2 * HALO, D), BF16)
            + 2 * _nbytes((tch, D), BF16) + 2 * _nbytes((2, tch, tp), F32)
            + 8 * _nbytes((tch, tp), F32))
    return pl.pallas_call(
        functools.partial(_inproj_cm_kernel, tp=tp, tch=tch, n_conv=n_conv),
        grid=(P, NB, n_out * per),
        in_specs=[pl.BlockSpec((1, 2, tp, D), lambda p, i, c: (p, 0, i, 0)),
                  pl.BlockSpec((1, 2, HALO, D), lambda p, i, c: (p, 0, jnp.maximum(i * nh - 1, 0), 0)),
                  pl.BlockSpec((1, 2, HALO, D),
                               lambda p, i, c: (p, 0, jnp.minimum((i + 1) * nh, L // HALO - 1), 0)),
                  pl.BlockSpec((1, D), lambda p, i, c: (0, 0)),
                  pl.BlockSpec((tch, D), lambda p, i, c: (c, 0)),
                  pl.BlockSpec((tch, 3), lambda p, i, c: (jnp.minimum(c, n_conv - 1), 0))],
        out_specs=pl.BlockSpec((1, 1, 2, 1, tch * SUB, LANES),
                               lambda p, i, c: (c // per, p, 0, i, c % per, 0)),
        out_shape=jax.ShapeDtypeStruct((n_out, P, 2, NB, C * SUB, LANES), F32),
        scratch_shapes=[pltpu.VMEM((2, D, tp), BF16), pltpu.VMEM((2, 2 * HALO, D), BF16)],
        compiler_params=_params(vmem, ("parallel", "parallel", "arbitrary")),
        name="inproj_cm",
    )(xv, xv, xv, g_pre.reshape(1, D), wht, wsht)


def _fourier_a_kernel(z_ref, fa_ref, a_ref, *, tt):
    for tl in range(tt):
        x = z_ref[0, :, :, tl]
        y = _dot(fa_ref[tl], x.reshape(-1, x.shape[-1]))
        a_ref[0, :, :, tl] = y.astype(BF16).reshape(x.shape)


def _fourier_a(z, fa, *, tt):
    B, _, _, T, _, C = z.shape
    blk = (1, 2, FA, tt, TILE, C)
    vmem = 4 * _nbytes(blk, BF16) + 2 * _nbytes((tt,) + fa.shape[1:], BF16) + 4 * _nbytes((2 * FA * TILE, C), F32)
    return pl.pallas_call(
        functools.partial(_fourier_a_kernel, tt=tt),
        grid=(T // tt, B),
        in_specs=[pl.BlockSpec(blk, lambda t, b: (b, 0, 0, t, 0, 0)),
                  pl.BlockSpec((tt,) + fa.shape[1:], lambda t, b: (t, 0, 0))],
        out_specs=pl.BlockSpec(blk, lambda t, b: (b, 0, 0, t, 0, 0)),
        out_shape=jax.ShapeDtypeStruct(z.shape, BF16),
        compiler_params=_params(vmem, ("parallel", "parallel")),
        name="fourier_a",
    )(z, fa)


def _fourier_b_kernel(a_ref, fb_ref, o_ref, *, scale):
    a = a_ref[0, :, 0]
    fz = _dot(fb_ref[...], a.reshape(-1, a.shape[-1])) * scale
    o_ref[0, 0] = fz.astype(BF16)


def _fourier_b(a, fb, *, scale):
    B, _, _, FB, C = a.shape
    vmem = 2 * 3 * _nbytes((FB, C), BF16) + 2 * _nbytes(fb.shape, BF16) + 3 * _nbytes((FB, C), F32)
    return pl.pallas_call(
        functools.partial(_fourier_b_kernel, scale=scale),
        grid=(B, FA),
        in_specs=[pl.BlockSpec((1, 2, 1, FB, C), lambda b, k: (b, 0, k, 0, 0)),
                  pl.BlockSpec(fb.shape, lambda b, k: (0, 0))],
        out_specs=pl.BlockSpec((1, 1, FB, C), lambda b, k: (b, k, 0, 0)),
        out_shape=jax.ShapeDtypeStruct((B, FA, FB, C), BF16),
        compiler_params=_params(vmem, ("parallel", "parallel")),
        name="fourier_b",
    )(a, fb)


def _gather_group(load, nb):
    rows = [jnp.concatenate([load(i, j) for j in range(GROUP)], axis=1) for i in range(nb)]
    return jnp.concatenate(rows, axis=0)


def _to_rows(ar, ai):
    R = ar.shape[0]
    parts = [jnp.concatenate([ar[:, j * R:(j + 1) * R], ai[:, j * R:(j + 1) * R]], axis=1)
             for j in range(GROUP)]
    return jnp.concatenate(parts, axis=0)


def _to_lanes(z):
    R = z.shape[1] // 2
    zr = jnp.concatenate([z[j * R:(j + 1) * R, :R] for j in range(GROUP)], axis=1)
    zi = jnp.concatenate([z[j * R:(j + 1) * R, R:] for j in range(GROUP)], axis=1)
    return zr, zi


def _forward(a, tr, ti, mb):
    R = a.shape[0] // 2
    ar, ai = a[:R].astype(BF16), a[R:].astype(BF16)
    return _dot(_to_rows(ar * tr - ai * ti, ar * ti + ai * tr), mb)


def _filter_gen_kernel(bands_ref, w1_ref, b1_ref, f1_ref, w2_ref, b2_ref, f2_ref, w3_ref,
                       dec_ref, k_ref, s_ref, *, tp, tch, L):
    i = pl.program_id(0)
    n = i * tp + lax.broadcasted_iota(jnp.int32, (1, tp), 1)
    t = jnp.where(n < L, n, 2 * L - 1 - n).astype(F32)
    tnorm = t / (L - 1)
    ang = (2.0 * math.pi * t) / L
    row = lax.broadcasted_iota(jnp.int32, (LANES, tp), 0)
    arg = bands_ref[...] * ang
    feats = jnp.where(row == 0, tnorm,
                      jnp.where(row <= FILT_BANDS, jnp.cos(arg),
                                jnp.where(row <= 2 * FILT_BANDS, -jnp.sin(arg), 0.0)))
    hdot = lambda a, b: jnp.dot(a, b, precision=HIGHEST, preferred_element_type=F32)
    h = jnp.sin(f1_ref[...] * (hdot(w1_ref[...], feats) + b1_ref[...]))
    h = jnp.sin(f2_ref[...] * (hdot(w2_ref[...], h) + b2_ref[...]))

    @pl.when(i == 0)
    def _():
        s_ref[...] = jnp.zeros_like(s_ref)

    for c in range(w3_ref.shape[0] // tch):
        rows = pl.ds(c * tch, tch)
        hc = hdot(w3_ref[rows, :], h) * jnp.exp(-tnorm * jnp.abs(dec_ref[rows, :]))
        s_ref[rows, :] += jnp.sum(jnp.abs(hc), axis=1, keepdims=True)
        hc = jnp.where(n == L, 0.0, hc)
        for q in range(tp // LANES):
            k_ref[0, pl.ds(c * tch * SUB + q, tch, stride=SUB), :] = hc[:, q * LANES:(q + 1) * LANES]


def _filter_gen(L, C, w1, b1, f1, w2, b2, f2, w3, decay, *, tch):
    N = 2 * L
    tp = SUB * LANES
    emb, hid = w1.shape
    W = 2 * C
    bands = jnp.linspace(1e-4, FILT_BANDS - 1, FILT_BANDS, dtype=F32)
    col = jnp.zeros((LANES, 1), F32).at[1:1 + FILT_BANDS, 0].set(bands)
    col = col.at[1 + FILT_BANDS:1 + 2 * FILT_BANDS, 0].set(bands)
    w1t = jnp.zeros((hid, LANES), F32).at[:, :emb].set(w1.T)
    per_dir = L // tp
    const = lambda i: (0, 0)
    vmem = (2 * _nbytes((W * SUB, LANES), F32) + 2 * _nbytes((W, LANES), F32) * 3
            + 6 * _nbytes((tch, tp), F32) + 8 * _nbytes((LANES, tp), F32))
    return pl.pallas_call(
        functools.partial(_filter_gen_kernel, tp=tp, tch=tch, L=L),
        grid=(N // tp,),
        in_specs=[pl.BlockSpec((LANES, 1), const),
                  pl.BlockSpec((hid, LANES), const), pl.BlockSpec((hid, 1), const),
                  pl.BlockSpec((hid, 1), const),
                  pl.BlockSpec((hid, hid), const), pl.BlockSpec((hid, 1), const),
                  pl.BlockSpec((hid, 1), const),
                  pl.BlockSpec((W, hid), lambda i: (i // per_dir, 0)),
                  pl.BlockSpec((W, 1), lambda i: (i // per_dir, 0))],
        out_specs=[pl.BlockSpec((1, W * SUB, LANES), lambda i: (i, 0, 0)),
                   pl.BlockSpec((W, 1), const)],
        out_shape=[jax.ShapeDtypeStruct((N // tp, W * SUB, LANES), F32),
                   jax.ShapeDtypeStruct((W, 1), F32)],
        compiler_params=_params(vmem, ("arbitrary",)),
        name="filter_gen",
    )(col, w1t, b1.reshape(hid, 1), f1.reshape(hid, 1), w2.T, b2.reshape(hid, 1),
      f2.reshape(hid, 1), w3.T, decay.reshape(2 * W, 1))


def _filter_spec_kernel(k_ref, s_ref, fk_ref, tw_ref, mb_ref, kf_ref, *, tcf, R):
    nb = k_ref.shape[0]

    def body(g, carry):
        r0 = pl.multiple_of(g * GROUP * SUB, GROUP * SUB)
        x = _gather_group(lambda i, j: k_ref[i, pl.ds(r0 + j * SUB, SUB), :], nb)
        a = _dot(fk_ref[...], x.astype(BF16))
        z = _forward(a, tw_ref[0], tw_ref[1], mb_ref[...])
        c0 = pl.program_id(0) * tcf + g * GROUP
        inv = jnp.concatenate(
            [jnp.broadcast_to(1.0 / s_ref[pl.ds(c0 + j, 1), :], (R, 1)) for j in range(GROUP)], axis=0)
        kf_ref[pl.ds(pl.multiple_of(g * GROUP * R, GROUP * R), GROUP * R), :] = (z * inv).astype(BF16)
        return carry

    lax.fori_loop(0, tcf // GROUP, body, 0)


def _filter_spec(k, s, tabs, *, tcf):
    nb, rows, _ = k.shape
    W = rows // SUB
    R = LANES
    vmem = (2 * _nbytes((nb, tcf * SUB, LANES), F32) + 2 * _nbytes((tcf * R, 2 * R), BF16)
            + 2 * _nbytes((W, LANES), F32) + 16 * _nbytes((2 * R, GROUP * R), F32))
    return pl.pallas_call(
        functools.partial(_filter_spec_kernel, tcf=tcf, R=R),
        grid=(W // tcf,),
        in_specs=[pl.BlockSpec((nb, tcf * SUB, LANES), lambda c: (0, c, 0)),
                  pl.BlockSpec((W, 1), lambda c: (0, 0)),
                  pl.BlockSpec(tabs["fk"].shape, lambda c: (0, 0)),
                  pl.BlockSpec(tabs["tw"].shape, lambda c: (0, 0, 0)),
                  pl.BlockSpec(tabs["mb"].shape, lambda c: (0, 0))],
        out_specs=pl.BlockSpec((tcf * R, 2 * R), lambda c: (c, 0)),
        out_shape=jax.ShapeDtypeStruct((W * R, 2 * R), BF16),
        compiler_params=_params(vmem, ("parallel",)),
        name="filter_spec",
    )(k, s, tabs["fk"], tabs["tw"], tabs["mb"])


def _hyena_kernel(v_ref, x1_ref, x2_ref, hg_ref, kf0_ref, kf1_ref, d_ref, f1_ref, i1_ref,
                  tw_ref, mb_ref, mbi_ref, o_ref, *, tcc, R):
    cb = pl.program_id(0)
    nb = v_ref.shape[3]
    H = R // 2

    def conv(x, kf):
        tr, ti = tw_ref[0], tw_ref[1]
        a = _dot(f1_ref[...], x.astype(BF16))
        z = _forward(a, tr, ti, mb_ref[...]).astype(BF16)
        zr, zi = z[:, :R], z[:, R:]
        kr, ki = kf[:, :R], kf[:, R:]
        y = jnp.concatenate([zr * kr - zi * ki, zr * ki + zi * kr], axis=1)
        br, bi = _to_lanes(_dot(y, mbi_ref[...]).astype(BF16))
        b = jnp.concatenate([br * tr + bi * ti, bi * tr - br * ti], axis=0)
        return _dot(i1_ref[...], b)

    def body(g, carry):
        r0 = pl.multiple_of(g * GROUP * SUB, GROUP * SUB)
        k0 = pl.multiple_of(g * GROUP * R, GROUP * R)

        def load(ref):
            halves = [_gather_group(lambda i, j, s=s: ref[0, 0, s, i, pl.ds(r0 + j * SUB, SUB), :], nb)
                      for s in range(2)]
            return jnp.concatenate(halves, axis=0)

        drow = cb * (tcc // GROUP) + g
        v = load(v_ref)
        z1 = load(x1_ref) * (conv(v, kf0_ref[pl.ds(k0, GROUP * R), :]) + d_ref[0, pl.ds(drow, 1), :] * v)
        z2 = load(x2_ref) * (conv(z1, kf1_ref[pl.ds(k0, GROUP * R), :]) + d_ref[1, pl.ds(drow, 1), :] * z1)
        u = z2 * load(hg_ref)
        for s in range(2):
            for i in range(nb):
                for j in range(GROUP):
                    o_ref[0, s, i, pl.ds(r0 + j * SUB, SUB), :] = (
                        u[s * H + i * SUB:s * H + (i + 1) * SUB, j * R:(j + 1) * R])
        return carry

    lax.fori_loop(0, tcc // GROUP, body, 0, unroll=4)


def _hyena(acts, kf, dl, tabs, *, tcc):
    _, P, _, NB, rows, _ = acts.shape
    C = rows // SUB
    R = LANES
    per = C // tcc
    act = lambda w: pl.BlockSpec((1, 1, 2, NB, tcc * SUB, LANES), lambda c, p, w=w: (w, p, 0, 0, c, 0))
    const2 = lambda c, p: (0, 0)
    const3 = lambda c, p: (0, 0, 0)
    vmem = (2 * 5 * _nbytes((2, NB, tcc * SUB, LANES), F32) + 2 * 2 * _nbytes((tcc * R, 2 * R), BF16)
            + 2 * _nbytes(dl.shape, F32) + 2 * _nbytes(tabs["tw"].shape, BF16)
            + 24 * _nbytes((2 * R, GROUP * R), F32))
    return pl.pallas_call(
        functools.partial(_hyena_kernel, tcc=tcc, R=R),
        grid=(per, P),
        in_specs=[act(0), act(1), act(2), act(3),
                  pl.BlockSpec((tcc * R, 2 * R), lambda c, p: (c, 0)),
                  pl.BlockSpec((tcc * R, 2 * R), lambda c, p: (per + c, 0)),
                  pl.BlockSpec(dl.shape, const3),
                  pl.BlockSpec(tabs["f1"].shape, const2),
                  pl.BlockSpec(tabs["i1"].shape, const2),
                  pl.BlockSpec(tabs["tw"].shape, const3),
                  pl.BlockSpec(tabs["mb"].shape, const2),
                  pl.BlockSpec(tabs["mbi"].shape, const2)],
        out_specs=pl.BlockSpec((1, 2, NB, tcc * SUB, LANES), lambda c, p: (p, 0, 0, c, 0)),
        out_shape=jax.ShapeDtypeStruct((P, 2, NB, rows, LANES), F32),
        compiler_params=_params(vmem, ("parallel", "parallel")),
        name="hyena",
    )(acts, acts, acts, acts, kf, kf, dl, tabs["f1"], tabs["i1"], tabs["tw"], tabs["mb"], tabs["mbi"])


def _merge_kernel(u_ref, fz_ref, sfg_ref, gf_ref, gh_ref, x_ref, perm_ref, wfo_ref, who_ref,
                  wout_ref, gpost_ref, o_ref, *, C, D, qq):
    q0 = pl.program_id(2) * qq
    us = [u_ref[0, s, 0, pl.ds(q0 + q, C, stride=SUB), :].T.astype(BF16)
          for s in range(2) for q in range(qq)]
    yh = _dot(jnp.concatenate(us, axis=0), who_ref[...])
    tp = qq * LANES
    nat = lambda ref: ref[0].reshape(2 * tp, ref.shape[-1])
    fz = jnp.concatenate([_dot(perm_ref[...], fz_ref[0, s].reshape(tp, C)) for s in range(2)], axis=0)
    yf = _dot((fz * nat(sfg_ref).astype(F32)).astype(BF16), wfo_ref[...])
    m = nat(gf_ref).astype(F32) * yf + nat(gh_ref).astype(F32) * yh
    out = _dot(m.astype(BF16), wout_ref[...])
    var = jnp.mean(out * out, axis=-1, keepdims=True)
    res = nat(x_ref) + out * lax.rsqrt(var + RMS_EPS) * gpost_ref[...]
    o_ref[0] = res.reshape(2, tp, D)


def _merge(u, fz, sfg, gf, gh, x, perm, wfo, who, wout, g_post):
    P, _, NB, rows, _ = u.shape
    C = rows // SUB
    B, L, D = x.shape
    tp = FA * TILE
    qq = tp // LANES
    pv = lambda t: t.reshape(P, 2, L, t.shape[-1])
    nat = lambda w: pl.BlockSpec((1, 2, tp, w), lambda p, i, q: (p, 0, i * (SUB // qq) + q, 0))
    const = lambda p, i, q: (0, 0)
    single = dict(pipeline_mode=pl.Buffered(1))
    weights = [perm, wfo, who, wout]
    vmem = (2 * _nbytes((2, rows, LANES), F32) + 2 * 4 * _nbytes((2, tp, D), BF16)
            + 4 * _nbytes((2, tp, D), F32) + sum(_nbytes(w.shape, BF16) for w in weights)
            + 8 * _nbytes((2 * tp, D), F32))
    out = pl.pallas_call(
        functools.partial(_merge_kernel, C=C, D=D, qq=qq),
        grid=(P, NB, SUB // qq),
        in_specs=[pl.BlockSpec((1, 2, 1, rows, LANES), lambda p, i, q: (p, 0, i, 0, 0)),
                  pl.BlockSpec((1, 2, FA, TILE, C), lambda p, i, q: (p, 0, 0, i * (SUB // qq) + q, 0)),
                  nat(C), nat(D), nat(D), nat(D)]
                 + [pl.BlockSpec(w.shape, const, **single) for w in weights]
                 + [pl.BlockSpec((1, D), const)],
        out_specs=nat(D),
        out_shape=jax.ShapeDtypeStruct((P, 2, L, D), F32),
        compiler_params=_params(vmem, ("parallel", "parallel", "arbitrary")),
        name="merge",
    )(u, fz.reshape((P, 2) + fz.shape[1:]), pv(sfg), pv(gf), pv(gh), pv(x), *weights,
      g_post.reshape(1, D))
    return out.reshape(B, L, D)


def _tile(n, want):
    t = min(n, want)
    while n % t:
        t //= 2
    return t


def _hyena_filters(L, C, tabs, w1, b1, f1, w2, b2, f2, w3, decay):
    k, s = _filter_gen(L, C, w1, b1, f1, w2, b2, f2, w3, decay, tch=_tile(2 * C, 512))
    return _filter_spec(k, s, tabs, tcf=_tile(2 * C, 32))


def _encoder_layer(x, kf, tabs, cd, dl, g_pre, wf, wht, wsht, wm, b_merge, wfo, who, wout, g_post):
    B, L, D = x.shape
    C = D
    FB = L // FA
    z, sfg, gf, gh = _inproj_nat(x, g_pre, wf, wm, b_merge, cd, tm=_tile(L, 512))
    acts = _inproj_cm(x, g_pre, wht, wsht, tch=_tile(C, 512))

    af = _fourier_a(z.reshape(B, 2, FA, FB // TILE, TILE, C), tabs["fa"], tt=4)
    scale = 1.0 / math.sqrt(L * (C // F_GROUPS))
    fz = _fourier_b(af.reshape(B, 2, FA, FB, C), tabs["fb"], scale=scale)

    u = _hyena(acts, kf, dl, tabs, tcc=_tile(C, 32))
    return _merge(u, fz, sfg, gf, gh, x, tabs["perm"], wfo, who, wout, g_post)


def kernel(x_prompt, x_sample, g_pre, w_in, w_short, filt_w1, filt_b1, filt_freq1, filt_w2,
           filt_b2, filt_freq2, filt_w3, filt_decay, hyena_d, w_fourier_out, w_hyena_out,
           b_merge, w_out, g_post):
    depth = g_pre.shape[0]
    L, D = x_prompt.shape[1], x_prompt.shape[2]
    assert x_sample.shape[1:] == (L, D)
    C = D
    R = LANES
    assert R * R == 2 * L and C % (GROUP * F_GROUPS) == 0
    assert x_prompt.shape[0] % 2 == 0 and x_sample.shape[0] % 2 == 0
    tabs = _tables(R)
    cd = _channel_dft(C // F_GROUPS)
    ys = [x_prompt, x_sample]
    for i in range(depth):
        w = w_in[i].astype(BF16)
        wf, wh, wm = w[:, :2 * C], w[:, 2 * C:6 * C], w[:, 6 * C:]
        taps = w_short[i].T
        dl = jnp.repeat(hyena_d[i], R, axis=-1).reshape(2, C // GROUP, GROUP * R)
        kf = _hyena_filters(L, C, tabs, filt_w1[i], filt_b1[i], filt_freq1[i], filt_w2[i],
                            filt_b2[i], filt_freq2[i], filt_w3[i], filt_decay[i])
        args = (kf, tabs, cd, dl, g_pre[i], wf, wh.T, taps, wm, b_merge[i],
                w_fourier_out[i].astype(BF16), w_hyena_out[i].astype(BF16),
                w_out[i].astype(BF16), g_post[i])
        ys = [_encoder_layer(y, *args) for y in ys]
    return tuple(ys)
```

```python
import functools
import math

import jax
import jax.numpy as jnp
from jax import lax
from jax.experimental import pallas as pl
from jax.experimental.pallas import tpu as pltpu

BF16 = jnp.bfloat16
F32 = jnp.float32

F_GROUPS = 4
RMS_EPS = 1e-6
FILT_BANDS = 16
TILE = 16
HALO = TILE
FA = 16
LANES = 128
SUB = 8
GROUP = 8
V7X_VMEM_BYTES = 64 * 1024 * 1024
VMEM_CAP = V7X_VMEM_BYTES - 8 * 1024 * 1024
HIGHEST = lax.Precision.HIGHEST


def _dot(a, b):
    return jnp.dot(a, b, preferred_element_type=F32)


def _dot_nt(a, b):
    return lax.dot_general(a, b, (((1,), (1,)), ((), ())), preferred_element_type=F32)


def _params(vmem_bytes, semantics):
    limit = min(VMEM_CAP, vmem_bytes + 8 * 1024 * 1024)
    return pltpu.CompilerParams(dimension_semantics=semantics, vmem_limit_bytes=int(limit))


def _nbytes(shape, dtype):
    return math.prod(shape) * jnp.dtype(dtype).itemsize


def _silu(x):
    return x * jax.nn.sigmoid(x)


def _cis(num, den):
    ang = (num % den).astype(F32) * (2.0 * math.pi / den)
    return jnp.cos(ang), -jnp.sin(ang)


def _block(rr, ri):
    top = jnp.concatenate([rr, -ri], axis=-1)
    bot = jnp.concatenate([ri, rr], axis=-1)
    return jnp.concatenate([top, bot], axis=-2)


def _tables(R):
    N = R * R
    Lh = N // 2
    H = R // 2
    i32 = jnp.int32
    a = jnp.arange(R, dtype=i32)
    wr, wi = _cis(a[:, None] * a[None, :], R)
    f1 = _block(wr[:, :H], wi[:, :H])
    i1 = _block(wr[:, :H].T, -wi[:, :H].T) * (1.0 / N)
    fk = jnp.concatenate([wr, wi], axis=0)
    mb = _block(wr, -wi)
    mbi = _block(wr, wi)
    tr, ti = _cis(a[:, None] * a[None, :], N)
    tw = jnp.stack([jnp.tile(tr, (1, GROUP)), jnp.tile(ti, (1, GROUP))])
    FB = Lh // FA
    t = jnp.arange(FB // TILE, dtype=i32)[:, None, None, None]
    k1 = jnp.arange(FA, dtype=i32)[None, :, None, None]
    n1 = jnp.arange(FA, dtype=i32)[None, None, :, None]
    j = jnp.arange(TILE, dtype=i32)[None, None, None, :]
    vr, vi = _cis(k1 * (FB * n1 + TILE * t + j), Lh)
    eye = jnp.eye(TILE, dtype=F32)
    expand = lambda v: jnp.einsum("tknj,jm->tkjnm", v, eye).reshape(FB // TILE, FA * TILE, FA * TILE)
    fa = _block(expand(vr), expand(vi))
    b = jnp.arange(FB, dtype=i32)
    br, bi = _cis(b[:, None] * b[None, :], FB)
    fb = jnp.concatenate([br, -bi], axis=1)
    perm = jnp.eye(FA * TILE, dtype=F32).reshape(FA, TILE, FA * TILE).swapaxes(0, 1).reshape(FA * TILE, FA * TILE)
    cast = lambda t: t.astype(BF16)
    return dict(f1=cast(f1), i1=cast(i1), fk=cast(fk), mb=cast(mb), mbi=cast(mbi), tw=cast(tw),
                fa=cast(fa), fb=cast(fb), perm=cast(perm))


def _channel_dft(gd):
    a = jnp.arange(gd, dtype=jnp.int32)
    cr, ci = _cis(a[:, None] * a[None, :], gd)
    return jnp.concatenate([cr, ci], axis=1).astype(BF16)


def _inproj_nat_kernel(x_ref, gpre_ref, wf_ref, wm_ref, bm_ref, cd_ref,
                       z_ref, sfg_ref, gf_ref, gh_ref, *, C, D):
    xx = x_ref[0]
    ms = jnp.mean(xx * xx, axis=-1, keepdims=True)
    xm = (xx * lax.rsqrt(ms + RMS_EPS) * gpre_ref[...]).astype(BF16)
    pf = _dot(xm, wf_ref[...])
    sfg_ref[0] = _silu(pf[:, C:]).astype(BF16)
    fv = pf[:, :C].astype(BF16)
    gd = C // F_GROUPS
    for gi in range(F_GROUPS):
        zz = _dot(fv[:, gi * gd:(gi + 1) * gd], cd_ref[...])
        z_ref[0, 0, :, gi * gd:(gi + 1) * gd] = zz[:, :gd].astype(BF16)
        z_ref[0, 1, :, gi * gd:(gi + 1) * gd] = zz[:, gd:].astype(BF16)
    gate = jax.nn.sigmoid(_dot(xm, wm_ref[...]) + bm_ref[...])
    gf_ref[0] = gate[:, :D].astype(BF16)
    gh_ref[0] = gate[:, D:].astype(BF16)


def _inproj_nat(x, g_pre, w, b_merge, cd, *, tm):
    B, L, D = x.shape
    C = D
    const = lambda b, i: (0, 0)
    single = dict(pipeline_mode=pl.Buffered(1))
    row = pl.BlockSpec((1, tm, C), lambda b, i: (b, i, 0))
    act = jax.ShapeDtypeStruct((B, L, C), BF16)
    vmem = (2 * _nbytes((tm, D), F32) + 2 * 5 * _nbytes((tm, C), BF16)
            + 2 * _nbytes((D, 2 * C), BF16) + 4 * _nbytes((tm, 2 * C), F32))
    last = w.shape[1] // (2 * C) - 1
    return pl.pallas_call(
        functools.partial(_inproj_nat_kernel, C=C, D=D),
        grid=(B, L // tm),
        in_specs=[pl.BlockSpec((1, tm, D), lambda b, i: (b, i, 0)),
                  pl.BlockSpec((1, D), const),
                  pl.BlockSpec((D, 2 * C), const, **single),
                  pl.BlockSpec((D, 2 * D), lambda b, i: (0, last), **single),
                  pl.BlockSpec((1, 2 * D), const),
                  pl.BlockSpec(cd.shape, const)],
        out_specs=[pl.BlockSpec((1, 2, tm, C), lambda b, i: (b, 0, i, 0)), row, row, row],
        out_shape=[jax.ShapeDtypeStruct((B, 2, L, C), BF16), act, act, act],
        compiler_params=_params(vmem, ("parallel", "parallel")),
        name="inproj_nat",
    )(x, g_pre.reshape(1, D), w, w, b_merge.reshape(1, 2 * D), cd)


def _inproj_cm_kernel(x_ref, xp_ref, xn_ref, gpre_ref, w_ref, wsh_ref, o_ref, xs_ref, xh_ref,
                      *, tp, tch, n_conv):
    i = pl.program_id(1)
    ch = pl.program_id(2)
    last = pl.num_programs(1) - 1
    gain = gpre_ref[...]

    def norm(xx):
        ms = jnp.mean(xx * xx, axis=-1, keepdims=True)
        return xx * lax.rsqrt(ms + RMS_EPS) * gain

    @pl.when(ch == 0)
    def _():
        for s in range(2):
            xs_ref[s] = norm(x_ref[0, s]).astype(BF16)
            xh_ref[s, :HALO] = (norm(xp_ref[0, s]) * (i > 0).astype(F32)).astype(BF16)
            xh_ref[s, HALO:] = (norm(xn_ref[0, s]) * (i < last).astype(F32)).astype(BF16)

    def store(k, s, y):
        for q in range(tp // LANES):
            o_ref[0, 0, s, 0, pl.ds(k * tch * SUB + q, tch, stride=SUB), :] = y[:, q * LANES:(q + 1) * LANES]

    nsub = w_ref.shape[0] // tch

    @pl.when(ch < n_conv)
    def _():
        lane = lax.broadcasted_iota(jnp.int32, (tch, tp), 1)
        for k in range(nsub):
            w = w_ref[k * tch:(k + 1) * tch, :]
            taps = wsh_ref[k * tch:(k + 1) * tch, :]
            phs = [_dot_nt(w, xh_ref[s]) for s in range(2)]
            for s in range(2):
                p = _dot_nt(w, xs_ref[s])
                ph = phs[s]
                left = jnp.where(lane == 0, ph[:, HALO - 1:HALO], pltpu.roll(p, 1, axis=1))
                right = jnp.where(lane == tp - 1, ph[:, HALO:HALO + 1], pltpu.roll(p, tp - 1, axis=1))
                store(k, s, taps[:, 0:1] * left + taps[:, 1:2] * p + taps[:, 2:3] * right)

    @pl.when(ch >= n_conv)
    def _():
        for k in range(nsub):
            w = w_ref[k * tch:(k + 1) * tch, :]
            for s in range(2):
                store(k, s, _silu(_dot_nt(w, xs_ref[s])))


def _inproj_cm(x, g_pre, wht, wsht, *, tch, tcs):
    B, L, D = x.shape
    C = D
    P = B // 2
    tp = SUB * LANES
    NB = L // tp
    n_out = wht.shape[0] // C
    per = C // tcs
    nh = tp // HALO
    n_conv = wsht.shape[0] // tcs
    xv = x.reshape(P, 2, L, D)
    vmem = (2 * _nbytes((2, tp, D), F32) + _nbytes((2, tp + 2 * HALO, D), BF16)
            + 2 * _nbytes((tcs, D), BF16) + 2 * _nbytes((2, tcs * SUB, LANES), F32)
            + 10 * _nbytes((tch, tp), F32))
    return pl.pallas_call(
        functools.partial(_inproj_cm_kernel, tp=tp, tch=tch, n_conv=n_conv),
        grid=(P, NB, n_out * per),
        in_specs=[pl.BlockSpec((1, 2, tp, D), lambda p, i, c: (p, 0, i, 0)),
                  pl.BlockSpec((1, 2, HALO, D), lambda p, i, c: (p, 0, jnp.maximum(i * nh - 1, 0), 0)),
                  pl.BlockSpec((1, 2, HALO, D),
                               lambda p, i, c: (p, 0, jnp.minimum((i + 1) * nh, L // HALO - 1), 0)),
                  pl.BlockSpec((1, D), lambda p, i, c: (0, 0)),
                  pl.BlockSpec((tcs, D), lambda p, i, c: (c, 0)),
                  pl.BlockSpec((tcs, 3), lambda p, i, c: (jnp.minimum(c, n_conv - 1), 0))],
        out_specs=pl.BlockSpec((1, 1, 2, 1, tcs * SUB, LANES),
                               lambda p, i, c: (c // per, p, 0, i, c % per, 0)),
        out_shape=jax.ShapeDtypeStruct((n_out, P, 2, NB, C * SUB, LANES), F32),
        scratch_shapes=[pltpu.VMEM((2, tp, D), BF16), pltpu.VMEM((2, 2 * HALO, D), BF16)],
        compiler_params=_params(vmem, ("parallel", "parallel", "arbitrary")),
        name="inproj_cm",
    )(xv, xv, xv, g_pre.reshape(1, D), wht, wsht)


def _fourier_a_kernel(z_ref, fa_ref, a_ref, *, tt):
    for tl in range(tt):
        x = z_ref[0, :, :, tl]
        y = _dot(fa_ref[tl], x.reshape(-1, x.shape[-1]))
        a_ref[0, :, :, tl] = y.astype(BF16).reshape(x.shape)


def _fourier_a(z, fa, *, tt):
    B, _, _, T, _, C = z.shape
    blk = (1, 2, FA, tt, TILE, C)
    vmem = 4 * _nbytes(blk, BF16) + 2 * _nbytes((tt,) + fa.shape[1:], BF16) + 4 * _nbytes((2 * FA * TILE, C), F32)
    return pl.pallas_call(
        functools.partial(_fourier_a_kernel, tt=tt),
        grid=(T // tt, B),
        in_specs=[pl.BlockSpec(blk, lambda t, b: (b, 0, 0, t, 0, 0)),
                  pl.BlockSpec((tt,) + fa.shape[1:], lambda t, b: (t, 0, 0))],
        out_specs=pl.BlockSpec(blk, lambda t, b: (b, 0, 0, t, 0, 0)),
        out_shape=jax.ShapeDtypeStruct(z.shape, BF16),
        compiler_params=_params(vmem, ("parallel", "parallel")),
        name="fourier_a",
    )(z, fa)


def _fourier_b_kernel(a_ref, fb_ref, o_ref, *, scale, kb):
    for q in range(kb):
        a = a_ref[0, :, q]
        fz = _dot(fb_ref[...], a.reshape(-1, a.shape[-1])) * scale
        o_ref[0, q] = fz.astype(BF16)


def _fourier_b(a, fb, *, scale, kb=2):
    B, _, _, FB, C = a.shape
    vmem = 2 * 3 * kb * _nbytes((FB, C), BF16) + 2 * _nbytes(fb.shape, BF16) + 4 * _nbytes((FB, C), F32)
    return pl.pallas_call(
        functools.partial(_fourier_b_kernel, scale=scale, kb=kb),
        grid=(B, FA // kb),
        in_specs=[pl.BlockSpec((1, 2, kb, FB, C), lambda b, k: (b, 0, k, 0, 0)),
                  pl.BlockSpec(fb.shape, lambda b, k: (0, 0))],
        out_specs=pl.BlockSpec((1, kb, FB, C), lambda b, k: (b, k, 0, 0)),
        out_shape=jax.ShapeDtypeStruct((B, FA, FB, C), BF16),
        compiler_params=_params(vmem, ("parallel", "parallel")),
        name="fourier_b",
    )(a, fb)


def _gather_group(load, nb):
    rows = [jnp.concatenate([load(i, j) for j in range(GROUP)], axis=1) for i in range(nb)]
    return jnp.concatenate(rows, axis=0)


def _to_rows(ar, ai):
    R = ar.shape[0]
    parts = [jnp.concatenate([ar[:, j * R:(j + 1) * R], ai[:, j * R:(j + 1) * R]], axis=1)
             for j in range(GROUP)]
    return jnp.concatenate(parts, axis=0)


def _to_lanes(z):
    R = z.shape[1] // 2
    zr = jnp.concatenate([z[j * R:(j + 1) * R, :R] for j in range(GROUP)], axis=1)
    zi = jnp.concatenate([z[j * R:(j + 1) * R, R:] for j in range(GROUP)], axis=1)
    return zr, zi


def _forward(a, tr, ti, mb):
    R = a.shape[0] // 2
    ar, ai = a[:R].astype(BF16), a[R:].astype(BF16)
    return _dot(_to_rows(ar * tr - ai * ti, ar * ti + ai * tr), mb)


def _filter_gen_kernel(bands_ref, w1_ref, b1_ref, f1_ref, w2_ref, b2_ref, f2_ref, w3_ref,
                       dec_ref, k_ref, s_ref, *, tp, tch, L):
    i = pl.program_id(0)
    n = i * tp + lax.broadcasted_iota(jnp.int32, (1, tp), 1)
    t = jnp.where(n < L, n, 2 * L - 1 - n).astype(F32)
    tnorm = t / (L - 1)
    ang = (2.0 * math.pi * t) / L
    row = lax.broadcasted_iota(jnp.int32, (LANES, tp), 0)
    arg = bands_ref[...] * ang
    feats = jnp.where(row == 0, tnorm,
                      jnp.where(row <= FILT_BANDS, jnp.cos(arg),
                                jnp.where(row <= 2 * FILT_BANDS, -jnp.sin(arg), 0.0)))
    hdot = lambda a, b: jnp.dot(a, b, precision=HIGHEST, preferred_element_type=F32)
    h = jnp.sin(f1_ref[...] * (hdot(w1_ref[...], feats) + b1_ref[...]))
    hb = jnp.sin(f2_ref[...] * (hdot(w2_ref[...], h) + b2_ref[...])).astype(BF16)

    @pl.when(i == 0)
    def _():
        s_ref[...] = jnp.zeros_like(s_ref)

    for c in range(w3_ref.shape[0] // tch):
        rows = pl.ds(c * tch, tch)
        hc = _dot(w3_ref[rows, :].astype(BF16), hb) * jnp.exp(-tnorm * jnp.abs(dec_ref[rows, :]))
        s_ref[rows, :] += jnp.sum(jnp.abs(hc), axis=1, keepdims=True)
        hc = jnp.where(n == L, 0.0, hc)
        for q in range(tp // LANES):
            k_ref[0, pl.ds(c * tch * SUB + q, tch, stride=SUB), :] = hc[:, q * LANES:(q + 1) * LANES]


def _filter_gen(L, C, w1, b1, f1, w2, b2, f2, w3, decay, *, tch):
    N = 2 * L
    tp = SUB * LANES
    emb, hid = w1.shape
    W = 2 * C
    bands = jnp.linspace(1e-4, FILT_BANDS - 1, FILT_BANDS, dtype=F32)
    col = jnp.zeros((LANES, 1), F32).at[1:1 + FILT_BANDS, 0].set(bands)
    col = col.at[1 + FILT_BANDS:1 + 2 * FILT_BANDS, 0].set(bands)
    w1t = jnp.zeros((hid, LANES), F32).at[:, :emb].set(w1.T)
    per_dir = L // tp
    const = lambda i: (0, 0)
    vmem = (2 * _nbytes((W * SUB, LANES), F32) + 2 * _nbytes((W, LANES), F32) * 3
            + 6 * _nbytes((tch, tp), F32) + 8 * _nbytes((LANES, tp), F32))
    return pl.pallas_call(
        functools.partial(_filter_gen_kernel, tp=tp, tch=tch, L=L),
        grid=(N // tp,),
        in_specs=[pl.BlockSpec((LANES, 1), const),
                  pl.BlockSpec((hid, LANES), const), pl.BlockSpec((hid, 1), const),
                  pl.BlockSpec((hid, 1), const),
                  pl.BlockSpec((hid, hid), const), pl.BlockSpec((hid, 1), const),
                  pl.BlockSpec((hid, 1), const),
                  pl.BlockSpec((W, hid), lambda i: (i // per_dir, 0)),
                  pl.BlockSpec((W, 1), lambda i: (i // per_dir, 0))],
        out_specs=[pl.BlockSpec((1, W * SUB, LANES), lambda i: (i, 0, 0)),
                   pl.BlockSpec((W, 1), const)],
        out_shape=[jax.ShapeDtypeStruct((N // tp, W * SUB, LANES), F32),
                   jax.ShapeDtypeStruct((W, 1), F32)],
        compiler_params=_params(vmem, ("arbitrary",)),
        name="filter_gen",
    )(col, w1t, b1.reshape(hid, 1), f1.reshape(hid, 1), w2.T, b2.reshape(hid, 1),
      f2.reshape(hid, 1), w3.T, decay.reshape(2 * W, 1))


def _filter_spec_kernel(k_ref, s_ref, fk_ref, tw_ref, mb_ref, kf_ref, *, tcf, R):
    nb = k_ref.shape[0]

    def body(g, carry):
        r0 = pl.multiple_of(g * GROUP * SUB, GROUP * SUB)
        x = _gather_group(lambda i, j: k_ref[i, pl.ds(r0 + j * SUB, SUB), :], nb)
        a = _dot(fk_ref[...], x.astype(BF16))
        z = _forward(a, tw_ref[0], tw_ref[1], mb_ref[...])
        c0 = pl.program_id(0) * tcf + g * GROUP
        inv = jnp.concatenate(
            [jnp.broadcast_to(1.0 / s_ref[pl.ds(c0 + j, 1), :], (R, 1)) for j in range(GROUP)], axis=0)
        kf_ref[pl.ds(pl.multiple_of(g * GROUP * R, GROUP * R), GROUP * R), :] = (z * inv).astype(BF16)
        return carry

    lax.fori_loop(0, tcf // GROUP, body, 0)


def _filter_spec(k, s, tabs, *, tcf):
    nb, rows, _ = k.shape
    W = rows // SUB
    R = LANES
    vmem = (2 * _nbytes((nb, tcf * SUB, LANES), F32) + 2 * _nbytes((tcf * R, 2 * R), BF16)
            + 2 * _nbytes((W, LANES), F32) + 16 * _nbytes((2 * R, GROUP * R), F32))
    return pl.pallas_call(
        functools.partial(_filter_spec_kernel, tcf=tcf, R=R),
        grid=(W // tcf,),
        in_specs=[pl.BlockSpec((nb, tcf * SUB, LANES), lambda c: (0, c, 0)),
                  pl.BlockSpec((W, 1), lambda c: (0, 0)),
                  pl.BlockSpec(tabs["fk"].shape, lambda c: (0, 0)),
                  pl.BlockSpec(tabs["tw"].shape, lambda c: (0, 0, 0)),
                  pl.BlockSpec(tabs["mb"].shape, lambda c: (0, 0))],
        out_specs=pl.BlockSpec((tcf * R, 2 * R), lambda c: (c, 0)),
        out_shape=jax.ShapeDtypeStruct((W * R, 2 * R), BF16),
        compiler_params=_params(vmem, ("parallel",)),
        name="filter_spec",
    )(k, s, tabs["fk"], tabs["tw"], tabs["mb"])


def _hyena_kernel(v_ref, x1_ref, x2_ref, hg_ref, kf0_ref, kf1_ref, d_ref, f1_ref, i1_ref,
                  tw_ref, mb_ref, mbi_ref, o_ref, *, tcc, R):
    cb = pl.program_id(0)
    nb = v_ref.shape[3]
    H = R // 2

    def conv(x, kf):
        tr, ti = tw_ref[0], tw_ref[1]
        a = _dot(f1_ref[...], x.astype(BF16))
        z = _forward(a, tr, ti, mb_ref[...]).astype(BF16)
        zr, zi = z[:, :R], z[:, R:]
        kr, ki = kf[:, :R], kf[:, R:]
        y = jnp.concatenate([zr * kr - zi * ki, zr * ki + zi * kr], axis=1)
        br, bi = _to_lanes(_dot(y, mbi_ref[...]).astype(BF16))
        b = jnp.concatenate([br * tr + bi * ti, bi * tr - br * ti], axis=0)
        return _dot(i1_ref[...], b)

    def body(g, carry):
        r0 = pl.multiple_of(g * GROUP * SUB, GROUP * SUB)
        k0 = pl.multiple_of(g * GROUP * R, GROUP * R)

        def load(ref):
            halves = [_gather_group(lambda i, j, s=s: ref[0, 0, s, i, pl.ds(r0 + j * SUB, SUB), :], nb)
                      for s in range(2)]
            return jnp.concatenate(halves, axis=0)

        drow = cb * (tcc // GROUP) + g
        v = load(v_ref)
        z1 = load(x1_ref) * (conv(v, kf0_ref[pl.ds(k0, GROUP * R), :]) + d_ref[0, pl.ds(drow, 1), :] * v)
        z2 = load(x2_ref) * (conv(z1, kf1_ref[pl.ds(k0, GROUP * R), :]) + d_ref[1, pl.ds(drow, 1), :] * z1)
        u = z2 * load(hg_ref)
        for s in range(2):
            for i in range(nb):
                for j in range(GROUP):
                    o_ref[0, s, i, pl.ds(r0 + j * SUB, SUB), :] = (
                        u[s * H + i * SUB:s * H + (i + 1) * SUB, j * R:(j + 1) * R])
        return carry

    lax.fori_loop(0, tcc // GROUP, body, 0, unroll=4)


def _hyena(acts, kf, dl, tabs, *, tcc):
    _, P, _, NB, rows, _ = acts.shape
    C = rows // SUB
    R = LANES
    per = C // tcc
    act = lambda w: pl.BlockSpec((1, 1, 2, NB, tcc * SUB, LANES), lambda c, p, w=w: (w, p, 0, 0, c, 0))
    const2 = lambda c, p: (0, 0)
    const3 = lambda c, p: (0, 0, 0)
    vmem = (2 * 5 * _nbytes((2, NB, tcc * SUB, LANES), F32) + 2 * 2 * _nbytes((tcc * R, 2 * R), BF16)
            + 2 * _nbytes(dl.shape, F32) + 2 * _nbytes(tabs["tw"].shape, BF16)
            + 24 * _nbytes((2 * R, GROUP * R), F32))
    return pl.pallas_call(
        functools.partial(_hyena_kernel, tcc=tcc, R=R),
        grid=(per, P),
        in_specs=[act(0), act(1), act(2), act(3),
                  pl.BlockSpec((tcc * R, 2 * R), lambda c, p: (c, 0)),
                  pl.BlockSpec((tcc * R, 2 * R), lambda c, p: (per + c, 0)),
                  pl.BlockSpec(dl.shape, const3),
                  pl.BlockSpec(tabs["f1"].shape, const2),
                  pl.BlockSpec(tabs["i1"].shape, const2),
                  pl.BlockSpec(tabs["tw"].shape, const3),
                  pl.BlockSpec(tabs["mb"].shape, const2),
                  pl.BlockSpec(tabs["mbi"].shape, const2)],
        out_specs=pl.BlockSpec((1, 2, NB, tcc * SUB, LANES), lambda c, p: (p, 0, 0, c, 0)),
        out_shape=jax.ShapeDtypeStruct((P, 2, NB, rows, LANES), F32),
        compiler_params=_params(vmem, ("parallel", "parallel")),
        name="hyena",
    )(acts, acts, acts, acts, kf, kf, dl, tabs["f1"], tabs["i1"], tabs["tw"], tabs["mb"], tabs["mbi"])


def _merge_kernel(u_ref, fz_ref, sfg_ref, gf_ref, gh_ref, x_ref, perm_ref, wfo_ref, who_ref,
                  wout_ref, gpost_ref, o_ref, *, C, D, qq):
    q0 = pl.program_id(2) * qq
    us = [u_ref[0, s, 0, pl.ds(q0 + q, C, stride=SUB), :].T.astype(BF16)
          for s in range(2) for q in range(qq)]
    yh = _dot(jnp.concatenate(us, axis=0), who_ref[...])
    tp = qq * LANES
    nat = lambda ref: ref[0].reshape(2 * tp, ref.shape[-1])
    fz = jnp.concatenate([_dot(perm_ref[...], fz_ref[0, s].reshape(tp, C)) for s in range(2)], axis=0)
    yf = _dot((fz * nat(sfg_ref).astype(F32)).astype(BF16), wfo_ref[...])
    m = nat(gf_ref).astype(F32) * yf + nat(gh_ref).astype(F32) * yh
    out = _dot(m.astype(BF16), wout_ref[...])
    var = jnp.mean(out * out, axis=-1, keepdims=True)
    res = nat(x_ref) + out * lax.rsqrt(var + RMS_EPS) * gpost_ref[...]
    o_ref[0] = res.reshape(2, tp, D)


def _merge(u, fz, sfg, gf, gh, x, perm, wfo, who, wout, g_post):
    P, _, NB, rows, _ = u.shape
    C = rows // SUB
    B, L, D = x.shape
    tp = FA * TILE
    qq = tp // LANES
    pv = lambda t: t.reshape(P, 2, L, t.shape[-1])
    nat = lambda w: pl.BlockSpec((1, 2, tp, w), lambda p, i, q: (p, 0, i * (SUB // qq) + q, 0))
    const = lambda p, i, q: (0, 0)
    single = dict(pipeline_mode=pl.Buffered(1))
    weights = [perm, wfo, who, wout]
    vmem = (2 * _nbytes((2, rows, LANES), F32) + 2 * 4 * _nbytes((2, tp, D), BF16)
            + 4 * _nbytes((2, tp, D), F32) + sum(_nbytes(w.shape, BF16) for w in weights)
            + 8 * _nbytes((2 * tp, D), F32))
    out = pl.pallas_call(
        functools.partial(_merge_kernel, C=C, D=D, qq=qq),
        grid=(P, NB, SUB // qq),
        in_specs=[pl.BlockSpec((1, 2, 1, rows, LANES), lambda p, i, q: (p, 0, i, 0, 0)),
                  pl.BlockSpec((1, 2, FA, TILE, C), lambda p, i, q: (p, 0, 0, i * (SUB // qq) + q, 0)),
                  nat(C), nat(D), nat(D), nat(D)]
                 + [pl.BlockSpec(w.shape, const, **single) for w in weights]
                 + [pl.BlockSpec((1, D), const)],
        out_specs=nat(D),
        out_shape=jax.ShapeDtypeStruct((P, 2, L, D), F32),
        compiler_params=_params(vmem, ("parallel", "parallel", "arbitrary")),
        name="merge",
    )(u, fz.reshape((P, 2) + fz.shape[1:]), pv(sfg), pv(gf), pv(gh), pv(x), *weights,
      g_post.reshape(1, D))
    return out.reshape(B, L, D)


def _tile(n, want):
    t = min(n, want)
    while n % t:
        t //= 2
    return t


def _hyena_filters(L, C, tabs, w1, b1, f1, w2, b2, f2, w3, decay):
    k, s = _filter_gen(L, C, w1, b1, f1, w2, b2, f2, w3, decay, tch=_tile(2 * C, 512))
    return _filter_spec(k, s, tabs, tcf=_tile(2 * C, 32))


def _encoder_layer(x, kf, tabs, cd, dl, g_pre, w, wht, wsht, b_merge, wfo, who, wout, g_post):
    B, L, D = x.shape
    C = D
    FB = L // FA
    z, sfg, gf, gh = _inproj_nat(x, g_pre, w, b_merge, cd, tm=_tile(L, 512))
    acts = _inproj_cm(x, g_pre, wht, wsht, tch=_tile(C, 512), tcs=_tile(C, 512))

    af = _fourier_a(z.reshape(B, 2, FA, FB // TILE, TILE, C), tabs["fa"], tt=4)
    scale = 1.0 / math.sqrt(L * (C // F_GROUPS))
    fz = _fourier_b(af.reshape(B, 2, FA, FB, C), tabs["fb"], scale=scale)

    u = _hyena(acts, kf, dl, tabs, tcc=_tile(C, 32))
    return _merge(u, fz, sfg, gf, gh, x, tabs["perm"], wfo, who, wout, g_post)


def kernel(x_prompt, x_sample, g_pre, w_in, w_short, filt_w1, filt_b1, filt_freq1, filt_w2,
           filt_b2, filt_freq2, filt_w3, filt_decay, hyena_d, w_fourier_out, w_hyena_out,
           b_merge, w_out, g_post):
    depth = g_pre.shape[0]
    L, D = x_prompt.shape[1], x_prompt.shape[2]
    assert x_sample.shape[1:] == (L, D)
    C = D
    R = LANES
    assert R * R == 2 * L and C % (GROUP * F_GROUPS) == 0
    assert x_prompt.shape[0] % 2 == 0 and x_sample.shape[0] % 2 == 0
    tabs = _tables(R)
    cd = _channel_dft(C // F_GROUPS)
    ys = [x_prompt, x_sample]
    for i in range(depth):
        w = w_in[i].astype(BF16)
        wht = w_in[i, :, 2 * C:6 * C].T.astype(BF16)
        taps = w_short[i].T
        dl = jnp.repeat(hyena_d[i], R, axis=-1).reshape(2, C // GROUP, GROUP * R)
        kf = _hyena_filters(L, C, tabs, filt_w1[i], filt_b1[i], filt_freq1[i], filt_w2[i],
                            filt_b2[i], filt_freq2[i], filt_w3[i], filt_decay[i])
        args = (kf, tabs, cd, dl, g_pre[i], w, wht, taps, b_merge[i],
                w_fourier_out[i].astype(BF16), w_hyena_out[i].astype(BF16),
                w_out[i].astype(BF16), g_post[i])
        ys = [_encoder_layer(y, *args) for y in ys]
    return tuple(ys)
```

```python
import functools
import math

import jax
import jax.numpy as jnp
from jax import lax
from jax.experimental import pallas as pl
from jax.experimental.pallas import tpu as pltpu

BF16 = jnp.bfloat16
F32 = jnp.float32

F_GROUPS = 4
RMS_EPS = 1e-6
FILT_BANDS = 16
TILE = 16
HALO = TILE
FA = 16
LANES = 128
SUB = 8
MXU_COLS = 256
GROUP = 8
V7X_VMEM_BYTES = 64 * 1024 * 1024
VMEM_CAP = V7X_VMEM_BYTES - 8 * 1024 * 1024
HIGHEST = lax.Precision.HIGHEST


def _dot(a, b):
    return jnp.dot(a, b, preferred_element_type=F32)


def _params(vmem_bytes, semantics):
    limit = min(VMEM_CAP, vmem_bytes + 8 * 1024 * 1024)
    return pltpu.CompilerParams(dimension_semantics=semantics, vmem_limit_bytes=int(limit))


def _nbytes(shape, dtype):
    return math.prod(shape) * jnp.dtype(dtype).itemsize


def _silu(x):
    return x * jax.nn.sigmoid(x)


def _cis(num, den):
    ang = (num % den).astype(F32) * (2.0 * math.pi / den)
    return jnp.cos(ang), -jnp.sin(ang)


def _block(rr, ri):
    top = jnp.concatenate([rr, -ri], axis=-1)
    bot = jnp.concatenate([ri, rr], axis=-1)
    return jnp.concatenate([top, bot], axis=-2)


def _tables(R):
    N = R * R
    Lh = N // 2
    H = R // 2
    i32 = jnp.int32
    a = jnp.arange(R, dtype=i32)
    wr, wi = _cis(a[:, None] * a[None, :], R)
    f1 = _block(wr[:, :H], wi[:, :H])
    i1 = _block(wr[:, :H].T, -wi[:, :H].T) * (1.0 / N)
    fk = jnp.concatenate([wr, wi], axis=0)
    mb = _block(wr, -wi)
    mbi = _block(wr, wi)
    tr, ti = _cis(a[:, None] * a[None, :], N)
    tw = jnp.stack([jnp.tile(tr, (1, GROUP)), jnp.tile(ti, (1, GROUP))])
    FB = Lh // FA
    t = jnp.arange(FB // TILE, dtype=i32)[:, None, None, None]
    k1 = jnp.arange(FA, dtype=i32)[None, :, None, None]
    n1 = jnp.arange(FA, dtype=i32)[None, None, :, None]
    j = jnp.arange(TILE, dtype=i32)[None, None, None, :]
    vr, vi = _cis(k1 * (FB * n1 + TILE * t + j), Lh)
    eye = jnp.eye(TILE, dtype=F32)
    expand = lambda v: jnp.einsum("tknj,jm->tkjnm", v, eye).reshape(FB // TILE, FA * TILE, FA * TILE)
    fa = _block(expand(vr), expand(vi))
    b = jnp.arange(FB, dtype=i32)
    br, bi = _cis(b[:, None] * b[None, :], FB)
    fb = jnp.concatenate([br, -bi], axis=1)
    perm = jnp.eye(FA * TILE, dtype=F32).reshape(FA, TILE, FA * TILE).swapaxes(0, 1).reshape(FA * TILE, FA * TILE)
    cast = lambda t: t.astype(BF16)
    return dict(f1=cast(f1), i1=cast(i1), fk=cast(fk), mb=cast(mb), mbi=cast(mbi), tw=cast(tw),
                fa=cast(fa), fb=cast(fb), perm=cast(perm))


def _channel_dft(gd):
    a = jnp.arange(gd, dtype=jnp.int32)
    cr, ci = _cis(a[:, None] * a[None, :], gd)
    return jnp.concatenate([cr, ci], axis=1).astype(BF16)


def _inproj_nat_kernel(x_ref, gpre_ref, wf_ref, whg_ref, wm_ref, bm_ref, cd_ref,
                       z_ref, sfg_ref, shg_ref, gf_ref, gh_ref, *, C, D):
    xx = x_ref[0]
    ms = jnp.mean(xx * xx, axis=-1, keepdims=True)
    xm = (xx * lax.rsqrt(ms + RMS_EPS) * gpre_ref[...]).astype(BF16)
    shg_ref[0] = _silu(_dot(xm, whg_ref[...])).astype(BF16)
    pf = _dot(xm, wf_ref[...])
    sfg_ref[0] = _silu(pf[:, C:]).astype(BF16)
    fv = pf[:, :C].astype(BF16)
    gd = C // F_GROUPS
    for gi in range(F_GROUPS):
        zz = _dot(fv[:, gi * gd:(gi + 1) * gd], cd_ref[...])
        z_ref[0, 0, :, gi * gd:(gi + 1) * gd] = zz[:, :gd].astype(BF16)
        z_ref[0, 1, :, gi * gd:(gi + 1) * gd] = zz[:, gd:].astype(BF16)
    gate = jax.nn.sigmoid(_dot(xm, wm_ref[...]) + bm_ref[...])
    gf_ref[0] = gate[:, :D].astype(BF16)
    gh_ref[0] = gate[:, D:].astype(BF16)


def _inproj_nat(x, g_pre, w, b_merge, cd, *, tm):
    B, L, D = x.shape
    C = D
    const = lambda b, i: (0, 0)
    single = dict(pipeline_mode=pl.Buffered(1))
    row = pl.BlockSpec((1, tm, C), lambda b, i: (b, i, 0))
    act = jax.ShapeDtypeStruct((B, L, C), BF16)
    vmem = (2 * _nbytes((tm, D), F32) + 2 * 6 * _nbytes((tm, C), BF16)
            + 5 * _nbytes((D, C), BF16) + 5 * _nbytes((tm, 2 * C), F32))
    hg_col = w.shape[1] // C - 3
    last = w.shape[1] // (2 * C) - 1
    return pl.pallas_call(
        functools.partial(_inproj_nat_kernel, C=C, D=D),
        grid=(B, L // tm),
        in_specs=[pl.BlockSpec((1, tm, D), lambda b, i: (b, i, 0)),
                  pl.BlockSpec((1, D), const),
                  pl.BlockSpec((D, 2 * C), const, **single),
                  pl.BlockSpec((D, C), lambda b, i: (0, hg_col), **single),
                  pl.BlockSpec((D, 2 * D), lambda b, i: (0, last), **single),
                  pl.BlockSpec((1, 2 * D), const),
                  pl.BlockSpec(cd.shape, const)],
        out_specs=[pl.BlockSpec((1, 2, tm, C), lambda b, i: (b, 0, i, 0)), row, row, row, row],
        out_shape=[jax.ShapeDtypeStruct((B, 2, L, C), BF16), act, act, act, act],
        compiler_params=_params(vmem, ("parallel", "parallel")),
        name="inproj_nat",
    )(x, g_pre.reshape(1, D), w, w, w, b_merge.reshape(1, 2 * D), cd)


def _inproj_cm_kernel(x_ref, xp_ref, xn_ref, gpre_ref, w_ref, wsh_ref, o_ref, xe_ref, *pe_refs,
                      tp, tch):
    i = pl.program_id(1)
    ch = pl.program_id(2)
    last = pl.num_programs(1) - 1
    gain = gpre_ref[...]

    def norm(xx):
        ms = jnp.mean(xx * xx, axis=-1, keepdims=True)
        return xx * lax.rsqrt(ms + RMS_EPS) * gain

    @pl.when(ch == 0)
    def _():
        for s in range(2):
            xe_ref[s, :HALO] = (norm(xp_ref[0, s]) * (i > 0).astype(F32)).astype(BF16)
            xe_ref[s, HALO:HALO + tp] = norm(x_ref[0, s]).astype(BF16)
            xe_ref[s, HALO + tp:] = (norm(xn_ref[0, s]) * (i < last).astype(F32)).astype(BF16)

    nh = tch // MXU_COLS
    for s in range(2):
        for h in range(nh):
            cols = slice(h * MXU_COLS, (h + 1) * MXU_COLS)
            taps = wsh_ref[:, cols]
            pe_ref = pe_refs[s * nh + h]
            pe_ref[...] = _dot(xe_ref[s], w_ref[:, cols])
            conv = (pe_ref[pl.ds(HALO - 1, tp), :] * taps[0:1]
                    + pe_ref[pl.ds(HALO, tp), :] * taps[1:2]
                    + pe_ref[pl.ds(HALO + 1, tp), :] * taps[2:3])
            y = conv.T
            for q in range(tp // LANES):
                o_ref[0, 0, s, 0, pl.ds(h * MXU_COLS * SUB + q, MXU_COLS, stride=SUB), :] = (
                    y[:, q * LANES:(q + 1) * LANES])


def _inproj_cm(x, g_pre, w, w_short, *, tch):
    B, L, D = x.shape
    C = D
    P = B // 2
    tp = SUB * LANES
    NB = L // tp
    n_out = w_short.shape[1] // C
    per = C // tch
    first = 2 * C // tch
    nh = tp // HALO
    te = tp + 2 * HALO
    xv = x.reshape(P, 2, L, D)
    vmem = (2 * _nbytes((2, tp, D), F32) + _nbytes((2, te, D), BF16) + _nbytes((2, te, tch), F32)
            + 2 * _nbytes((D, tch), BF16) + 2 * _nbytes((2, tch * SUB, LANES), F32)
            + 8 * _nbytes((tch, tp), F32))
    return pl.pallas_call(
        functools.partial(_inproj_cm_kernel, tp=tp, tch=tch),
        grid=(P, NB, n_out * per),
        in_specs=[pl.BlockSpec((1, 2, tp, D), lambda p, i, c: (p, 0, i, 0)),
                  pl.BlockSpec((1, 2, HALO, D), lambda p, i, c: (p, 0, jnp.maximum(i * nh - 1, 0), 0)),
                  pl.BlockSpec((1, 2, HALO, D),
                               lambda p, i, c: (p, 0, jnp.minimum((i + 1) * nh, L // HALO - 1), 0)),
                  pl.BlockSpec((1, D), lambda p, i, c: (0, 0)),
                  pl.BlockSpec((D, tch), lambda p, i, c: (0, first + c)),
                  pl.BlockSpec((3, tch), lambda p, i, c: (0, c))],
        out_specs=pl.BlockSpec((1, 1, 2, 1, tch * SUB, LANES),
                               lambda p, i, c: (c // per, p, 0, i, c % per, 0)),
        out_shape=jax.ShapeDtypeStruct((n_out, P, 2, NB, C * SUB, LANES), F32),
        scratch_shapes=[pltpu.VMEM((2, te, D), BF16)]
                       + [pltpu.VMEM((te, MXU_COLS), F32)] * (2 * tch // MXU_COLS),
        compiler_params=_params(vmem, ("parallel", "parallel", "arbitrary")),
        name="inproj_cm",
    )(xv, xv, xv, g_pre.reshape(1, D), w, w_short)


def _fourier_a_kernel(z_ref, fa_ref, a_ref, *, tt):
    for tl in range(tt):
        x = z_ref[0, :, :, tl]
        y = _dot(fa_ref[tl], x.reshape(-1, x.shape[-1]))
        a_ref[0, :, :, tl] = y.astype(BF16).reshape(x.shape)


def _fourier_a(z, fa, *, tt):
    B, _, _, T, _, C = z.shape
    blk = (1, 2, FA, tt, TILE, C)
    vmem = 4 * _nbytes(blk, BF16) + 2 * _nbytes((tt,) + fa.shape[1:], BF16) + 4 * _nbytes((2 * FA * TILE, C), F32)
    return pl.pallas_call(
        functools.partial(_fourier_a_kernel, tt=tt),
        grid=(T // tt, B),
        in_specs=[pl.BlockSpec(blk, lambda t, b: (b, 0, 0, t, 0, 0)),
                  pl.BlockSpec((tt,) + fa.shape[1:], lambda t, b: (t, 0, 0))],
        out_specs=pl.BlockSpec(blk, lambda t, b: (b, 0, 0, t, 0, 0)),
        out_shape=jax.ShapeDtypeStruct(z.shape, BF16),
        compiler_params=_params(vmem, ("parallel", "parallel")),
        name="fourier_a",
    )(z, fa)


def _fourier_b_kernel(a_ref, fb_ref, o_ref, *, scale, kb):
    for q in range(kb):
        a = a_ref[0, :, q]
        fz = _dot(fb_ref[...], a.reshape(-1, a.shape[-1])) * scale
        o_ref[0, q] = fz.astype(BF16)


def _fourier_b(a, fb, *, scale, kb=2):
    B, _, _, FB, C = a.shape
    vmem = 2 * 3 * kb * _nbytes((FB, C), BF16) + 2 * _nbytes(fb.shape, BF16) + 4 * _nbytes((FB, C), F32)
    return pl.pallas_call(
        functools.partial(_fourier_b_kernel, scale=scale, kb=kb),
        grid=(B, FA // kb),
        in_specs=[pl.BlockSpec((1, 2, kb, FB, C), lambda b, k: (b, 0, k, 0, 0)),
                  pl.BlockSpec(fb.shape, lambda b, k: (0, 0))],
        out_specs=pl.BlockSpec((1, kb, FB, C), lambda b, k: (b, k, 0, 0)),
        out_shape=jax.ShapeDtypeStruct((B, FA, FB, C), BF16),
        compiler_params=_params(vmem, ("parallel", "parallel")),
        name="fourier_b",
    )(a, fb)


def _gather_group(load, nb):
    rows = [jnp.concatenate([load(i, j) for j in range(GROUP)], axis=1) for i in range(nb)]
    return jnp.concatenate(rows, axis=0)


def _to_rows(ar, ai):
    R = ar.shape[0]
    parts = [jnp.concatenate([ar[:, j * R:(j + 1) * R], ai[:, j * R:(j + 1) * R]], axis=1)
             for j in range(GROUP)]
    return jnp.concatenate(parts, axis=0)


def _to_lanes(z):
    R = z.shape[1] // 2
    zr = jnp.concatenate([z[j * R:(j + 1) * R, :R] for j in range(GROUP)], axis=1)
    zi = jnp.concatenate([z[j * R:(j + 1) * R, R:] for j in range(GROUP)], axis=1)
    return zr, zi


def _forward(a, tr, ti, mb):
    R = a.shape[0] // 2
    ar, ai = a[:R].astype(BF16), a[R:].astype(BF16)
    return _dot(_to_rows(ar * tr - ai * ti, ar * ti + ai * tr), mb)


def _filter_gen_kernel(bands_ref, w1_ref, b1_ref, f1_ref, w2_ref, b2_ref, f2_ref, w3_ref,
                       dec_ref, k_ref, s_ref, *, tp, tch, L):
    i = pl.program_id(0)
    n = i * tp + lax.broadcasted_iota(jnp.int32, (1, tp), 1)
    t = jnp.where(n < L, n, 2 * L - 1 - n).astype(F32)
    tnorm = t / (L - 1)
    ang = (2.0 * math.pi * t) / L
    row = lax.broadcasted_iota(jnp.int32, (LANES, tp), 0)
    arg = bands_ref[...] * ang
    feats = jnp.where(row == 0, tnorm,
                      jnp.where(row <= FILT_BANDS, jnp.cos(arg),
                                jnp.where(row <= 2 * FILT_BANDS, -jnp.sin(arg), 0.0)))
    hdot = lambda a, b: jnp.dot(a, b, precision=HIGHEST, preferred_element_type=F32)
    h = jnp.sin(f1_ref[...] * (hdot(w1_ref[...], feats) + b1_ref[...]))
    hb = jnp.sin(f2_ref[...] * (hdot(w2_ref[...], h) + b2_ref[...])).astype(BF16)

    @pl.when(i == 0)
    def _():
        s_ref[...] = jnp.zeros_like(s_ref)

    for c in range(w3_ref.shape[0] // tch):
        rows = pl.ds(c * tch, tch)
        hc = _dot(w3_ref[rows, :].astype(BF16), hb) * jnp.exp(-tnorm * jnp.abs(dec_ref[rows, :]))
        s_ref[rows, :] += jnp.sum(jnp.abs(hc), axis=1, keepdims=True)
        hc = jnp.where(n == L, 0.0, hc)
        for q in range(tp // LANES):
            k_ref[0, pl.ds(c * tch * SUB + q, tch, stride=SUB), :] = hc[:, q * LANES:(q + 1) * LANES]


def _filter_gen(L, C, w1, b1, f1, w2, b2, f2, w3, decay, *, tch):
    N = 2 * L
    tp = SUB * LANES
    emb, hid = w1.shape
    W = 2 * C
    bands = jnp.linspace(1e-4, FILT_BANDS - 1, FILT_BANDS, dtype=F32)
    col = jnp.zeros((LANES, 1), F32).at[1:1 + FILT_BANDS, 0].set(bands)
    col = col.at[1 + FILT_BANDS:1 + 2 * FILT_BANDS, 0].set(bands)
    w1t = jnp.zeros((hid, LANES), F32).at[:, :emb].set(w1.T)
    per_dir = L // tp
    const = lambda i: (0, 0)
    vmem = (2 * _nbytes((W * SUB, LANES), F32) + 2 * _nbytes((W, LANES), F32) * 3
            + 6 * _nbytes((tch, tp), F32) + 8 * _nbytes((LANES, tp), F32))
    return pl.pallas_call(
        functools.partial(_filter_gen_kernel, tp=tp, tch=tch, L=L),
        grid=(N // tp,),
        in_specs=[pl.BlockSpec((LANES, 1), const),
                  pl.BlockSpec((hid, LANES), const), pl.BlockSpec((hid, 1), const),
                  pl.BlockSpec((hid, 1), const),
                  pl.BlockSpec((hid, hid), const), pl.BlockSpec((hid, 1), const),
                  pl.BlockSpec((hid, 1), const),
                  pl.BlockSpec((W, hid), lambda i: (i // per_dir, 0)),
                  pl.BlockSpec((W, 1), lambda i: (i // per_dir, 0))],
        out_specs=[pl.BlockSpec((1, W * SUB, LANES), lambda i: (i, 0, 0)),
                   pl.BlockSpec((W, 1), const)],
        out_shape=[jax.ShapeDtypeStruct((N // tp, W * SUB, LANES), F32),
                   jax.ShapeDtypeStruct((W, 1), F32)],
        compiler_params=_params(vmem, ("arbitrary",)),
        name="filter_gen",
    )(col, w1t, b1.reshape(hid, 1), f1.reshape(hid, 1), w2.T, b2.reshape(hid, 1),
      f2.reshape(hid, 1), w3.T, decay.reshape(2 * W, 1))


def _filter_spec_kernel(k_ref, s_ref, fk_ref, tw_ref, mb_ref, kf_ref, *, tcf, R):
    nb = k_ref.shape[0]

    def body(g, carry):
        r0 = pl.multiple_of(g * GROUP * SUB, GROUP * SUB)
        x = _gather_group(lambda i, j: k_ref[i, pl.ds(r0 + j * SUB, SUB), :], nb)
        a = _dot(fk_ref[...], x.astype(BF16))
        z = _forward(a, tw_ref[0], tw_ref[1], mb_ref[...])
        c0 = pl.program_id(0) * tcf + g * GROUP
        inv = jnp.concatenate(
            [jnp.broadcast_to(1.0 / s_ref[pl.ds(c0 + j, 1), :], (R, 1)) for j in range(GROUP)], axis=0)
        kf_ref[pl.ds(pl.multiple_of(g * GROUP * R, GROUP * R), GROUP * R), :] = (z * inv).astype(BF16)
        return carry

    lax.fori_loop(0, tcf // GROUP, body, 0)


def _filter_spec(k, s, tabs, *, tcf):
    nb, rows, _ = k.shape
    W = rows // SUB
    R = LANES
    vmem = (2 * _nbytes((nb, tcf * SUB, LANES), F32) + 2 * _nbytes((tcf * R, 2 * R), BF16)
            + 2 * _nbytes((W, LANES), F32) + 16 * _nbytes((2 * R, GROUP * R), F32))
    return pl.pallas_call(
        functools.partial(_filter_spec_kernel, tcf=tcf, R=R),
        grid=(W // tcf,),
        in_specs=[pl.BlockSpec((nb, tcf * SUB, LANES), lambda c: (0, c, 0)),
                  pl.BlockSpec((W, 1), lambda c: (0, 0)),
                  pl.BlockSpec(tabs["fk"].shape, lambda c: (0, 0)),
                  pl.BlockSpec(tabs["tw"].shape, lambda c: (0, 0, 0)),
                  pl.BlockSpec(tabs["mb"].shape, lambda c: (0, 0))],
        out_specs=pl.BlockSpec((tcf * R, 2 * R), lambda c: (c, 0)),
        out_shape=jax.ShapeDtypeStruct((W * R, 2 * R), BF16),
        compiler_params=_params(vmem, ("parallel",)),
        name="filter_spec",
    )(k, s, tabs["fk"], tabs["tw"], tabs["mb"])


def _hyena_kernel(v_ref, x1_ref, x2_ref, kf0_ref, kf1_ref, d_ref, f1_ref, i1_ref,
                  tw_ref, mb_ref, mbi_ref, o_ref, *, tcc, R):
    cb = pl.program_id(0)
    nb = v_ref.shape[3]
    H = R // 2

    def conv(x, kf):
        tr, ti = tw_ref[0], tw_ref[1]
        a = _dot(f1_ref[...], x.astype(BF16))
        z = _forward(a, tr, ti, mb_ref[...]).astype(BF16)
        zr, zi = z[:, :R], z[:, R:]
        kr, ki = kf[:, :R], kf[:, R:]
        y = jnp.concatenate([zr * kr - zi * ki, zr * ki + zi * kr], axis=1)
        br, bi = _to_lanes(_dot(y, mbi_ref[...]).astype(BF16))
        b = jnp.concatenate([br * tr + bi * ti, bi * tr - br * ti], axis=0)
        return _dot(i1_ref[...], b)

    def body(g, carry):
        r0 = pl.multiple_of(g * GROUP * SUB, GROUP * SUB)
        k0 = pl.multiple_of(g * GROUP * R, GROUP * R)

        def load(ref):
            halves = [_gather_group(lambda i, j, s=s: ref[0, 0, s, i, pl.ds(r0 + j * SUB, SUB), :], nb)
                      for s in range(2)]
            return jnp.concatenate(halves, axis=0)

        drow = cb * (tcc // GROUP) + g
        v = load(v_ref)
        z1 = load(x1_ref) * (conv(v, kf0_ref[pl.ds(k0, GROUP * R), :]) + d_ref[0, pl.ds(drow, 1), :] * v)
        u = load(x2_ref) * (conv(z1, kf1_ref[pl.ds(k0, GROUP * R), :]) + d_ref[1, pl.ds(drow, 1), :] * z1)
        for s in range(2):
            for i in range(nb):
                for j in range(GROUP):
                    o_ref[0, s, i, pl.ds(r0 + j * SUB, SUB), :] = (
                        u[s * H + i * SUB:s * H + (i + 1) * SUB, j * R:(j + 1) * R])
        return carry

    lax.fori_loop(0, tcc // GROUP, body, 0, unroll=4)


def _hyena(acts, kf, dl, tabs, *, tcc):
    _, P, _, NB, rows, _ = acts.shape
    C = rows // SUB
    R = LANES
    per = C // tcc
    act = lambda w: pl.BlockSpec((1, 1, 2, NB, tcc * SUB, LANES), lambda c, p, w=w: (w, p, 0, 0, c, 0))
    const2 = lambda c, p: (0, 0)
    const3 = lambda c, p: (0, 0, 0)
    vmem = (2 * 4 * _nbytes((2, NB, tcc * SUB, LANES), F32) + 2 * 2 * _nbytes((tcc * R, 2 * R), BF16)
            + 2 * _nbytes(dl.shape, F32) + 2 * _nbytes(tabs["tw"].shape, BF16)
            + 24 * _nbytes((2 * R, GROUP * R), F32))
    return pl.pallas_call(
        functools.partial(_hyena_kernel, tcc=tcc, R=R),
        grid=(per, P),
        in_specs=[act(0), act(1), act(2),
                  pl.BlockSpec((tcc * R, 2 * R), lambda c, p: (c, 0)),
                  pl.BlockSpec((tcc * R, 2 * R), lambda c, p: (per + c, 0)),
                  pl.BlockSpec(dl.shape, const3),
                  pl.BlockSpec(tabs["f1"].shape, const2),
                  pl.BlockSpec(tabs["i1"].shape, const2),
                  pl.BlockSpec(tabs["tw"].shape, const3),
                  pl.BlockSpec(tabs["mb"].shape, const2),
                  pl.BlockSpec(tabs["mbi"].shape, const2)],
        out_specs=pl.BlockSpec((1, 2, NB, tcc * SUB, LANES), lambda c, p: (p, 0, 0, c, 0)),
        out_shape=jax.ShapeDtypeStruct((P, 2, NB, rows, LANES), F32),
        compiler_params=_params(vmem, ("parallel", "parallel")),
        name="hyena",
    )(acts, acts, acts, kf, kf, dl, tabs["f1"], tabs["i1"], tabs["tw"], tabs["mb"], tabs["mbi"])


def _merge_kernel(u_ref, fz_ref, sfg_ref, shg_ref, gf_ref, gh_ref, x_ref, perm_ref, wfo_ref, who_ref,
                  wout_ref, gpost_ref, o_ref, *, C, D, qq):
    q0 = pl.program_id(2) * qq
    tp = qq * LANES
    nat = lambda ref: ref[0].reshape(2 * tp, ref.shape[-1])
    us = [u_ref[0, s, 0, pl.ds(q0 + q, C, stride=SUB), :].T for s in range(2) for q in range(qq)]
    u = jnp.concatenate(us, axis=0) * nat(shg_ref).astype(F32)
    yh = _dot(u.astype(BF16), who_ref[...])
    fz = jnp.concatenate([_dot(perm_ref[...], fz_ref[0, s].reshape(tp, C)) for s in range(2)], axis=0)
    yf = _dot((fz * nat(sfg_ref).astype(F32)).astype(BF16), wfo_ref[...])
    m = nat(gf_ref).astype(F32) * yf + nat(gh_ref).astype(F32) * yh
    out = _dot(m.astype(BF16), wout_ref[...])
    var = jnp.mean(out * out, axis=-1, keepdims=True)
    res = nat(x_ref) + out * lax.rsqrt(var + RMS_EPS) * gpost_ref[...]
    o_ref[0] = res.reshape(2, tp, D)


def _merge(u, fz, sfg, shg, gf, gh, x, perm, wfo, who, wout, g_post):
    P, _, NB, rows, _ = u.shape
    C = rows // SUB
    B, L, D = x.shape
    tp = FA * TILE
    qq = tp // LANES
    pv = lambda t: t.reshape(P, 2, L, t.shape[-1])
    nat = lambda w: pl.BlockSpec((1, 2, tp, w), lambda p, i, q: (p, 0, i * (SUB // qq) + q, 0))
    const = lambda p, i, q: (0, 0)
    single = dict(pipeline_mode=pl.Buffered(1))
    weights = [perm, wfo, who, wout]
    vmem = (2 * _nbytes((2, rows, LANES), F32) + 2 * 5 * _nbytes((2, tp, D), BF16)
            + 4 * _nbytes((2, tp, D), F32) + sum(_nbytes(w.shape, BF16) for w in weights)
            + 8 * _nbytes((2 * tp, D), F32))
    out = pl.pallas_call(
        functools.partial(_merge_kernel, C=C, D=D, qq=qq),
        grid=(P, NB, SUB // qq),
        in_specs=[pl.BlockSpec((1, 2, 1, rows, LANES), lambda p, i, q: (p, 0, i, 0, 0)),
                  pl.BlockSpec((1, 2, FA, TILE, C), lambda p, i, q: (p, 0, 0, i * (SUB // qq) + q, 0)),
                  nat(C), nat(C), nat(D), nat(D), nat(D)]
                 + [pl.BlockSpec(w.shape, const, **single) for w in weights]
                 + [pl.BlockSpec((1, D), const)],
        out_specs=nat(D),
        out_shape=jax.ShapeDtypeStruct((P, 2, L, D), F32),
        compiler_params=_params(vmem, ("parallel", "parallel", "arbitrary")),
        name="merge",
    )(u, fz.reshape((P, 2) + fz.shape[1:]), pv(sfg), pv(shg), pv(gf), pv(gh), pv(x), *weights,
      g_post.reshape(1, D))
    return out.reshape(B, L, D)


def _tile(n, want):
    t = min(n, want)
    while n % t:
        t //= 2
    return t


def _hyena_filters(L, C, tabs, w1, b1, f1, w2, b2, f2, w3, decay):
    k, s = _filter_gen(L, C, w1, b1, f1, w2, b2, f2, w3, decay, tch=_tile(2 * C, 512))
    return _filter_spec(k, s, tabs, tcf=_tile(2 * C, 32))


def _encoder_layer(x, kf, tabs, cd, dl, g_pre, w, w_short, b_merge, wfo, who, wout, g_post):
    B, L, D = x.shape
    C = D
    FB = L // FA
    z, sfg, shg, gf, gh = _inproj_nat(x, g_pre, w, b_merge, cd, tm=_tile(L, 512))
    acts = _inproj_cm(x, g_pre, w, w_short, tch=_tile(C, 1024))

    af = _fourier_a(z.reshape(B, 2, FA, FB // TILE, TILE, C), tabs["fa"], tt=4)
    scale = 1.0 / math.sqrt(L * (C // F_GROUPS))
    fz = _fourier_b(af.reshape(B, 2, FA, FB, C), tabs["fb"], scale=scale)

    u = _hyena(acts, kf, dl, tabs, tcc=_tile(C, 32))
    return _merge(u, fz, sfg, shg, gf, gh, x, tabs["perm"], wfo, who, wout, g_post)


def kernel(x_prompt, x_sample, g_pre, w_in, w_short, filt_w1, filt_b1, filt_freq1, filt_w2,
           filt_b2, filt_freq2, filt_w3, filt_decay, hyena_d, w_fourier_out, w_hyena_out,
           b_merge, w_out, g_post):
    depth = g_pre.shape[0]
    L, D = x_prompt.shape[1], x_prompt.shape[2]
    assert x_sample.shape[1:] == (L, D)
    C = D
    R = LANES
    assert R * R == 2 * L and C % (GROUP * F_GROUPS) == 0
    assert x_prompt.shape[0] % 2 == 0 and x_sample.shape[0] % 2 == 0
    tabs = _tables(R)
    cd = _channel_dft(C // F_GROUPS)
    ys = [x_prompt, x_sample]
    for i in range(depth):
        w = w_in[i].astype(BF16)
        dl = jnp.repeat(hyena_d[i], R, axis=-1).reshape(2, C // GROUP, GROUP * R)
        kf = _hyena_filters(L, C, tabs, filt_w1[i], filt_b1[i], filt_freq1[i], filt_w2[i],
                            filt_b2[i], filt_freq2[i], filt_w3[i], filt_decay[i])
        args = (kf, tabs, cd, dl, g_pre[i], w, w_short[i], b_merge[i],
                w_fourier_out[i].astype(BF16), w_hyena_out[i].astype(BF16),
                w_out[i].astype(BF16), g_post[i])
        ys = [_encoder_layer(y, *args) for y in ys]
    return tuple(ys)
```

```python
import functools
import math

import jax
import jax.numpy as jnp
from jax import lax
from jax.experimental import pallas as pl
from jax.experimental.pallas import tpu as pltpu

BF16 = jnp.bfloat16
F32 = jnp.float32

F_GROUPS = 4
RMS_EPS = 1e-6
FILT_BANDS = 16
TILE = 16
HALO = TILE
FA = 16
LANES = 128
SUB = 8
MXU_COLS = 256
GROUP = 32
V7X_VMEM_BYTES = 64 * 1024 * 1024
VMEM_CAP = V7X_VMEM_BYTES - 8 * 1024 * 1024
HIGHEST = lax.Precision.HIGHEST


def _dot(a, b):
    return jnp.dot(a, b, preferred_element_type=F32)


def _params(vmem_bytes, semantics):
    limit = min(VMEM_CAP, vmem_bytes + 8 * 1024 * 1024)
    return pltpu.CompilerParams(dimension_semantics=semantics, vmem_limit_bytes=int(limit))


def _nbytes(shape, dtype):
    return math.prod(shape) * jnp.dtype(dtype).itemsize


def _silu(x):
    return x * jax.nn.sigmoid(x)


def _cis(num, den):
    ang = (num % den).astype(F32) * (2.0 * math.pi / den)
    return jnp.cos(ang), -jnp.sin(ang)


def _block(rr, ri):
    top = jnp.concatenate([rr, -ri], axis=-1)
    bot = jnp.concatenate([ri, rr], axis=-1)
    return jnp.concatenate([top, bot], axis=-2)


def _tables(R):
    N = R * R
    Lh = N // 2
    H = R // 2
    i32 = jnp.int32
    a = jnp.arange(R, dtype=i32)
    wr, wi = _cis(a[:, None] * a[None, :], R)
    f1 = _block(wr[:, :H], wi[:, :H])
    i1 = _block(wr[:, :H].T, -wi[:, :H].T) * (1.0 / N)
    fk = jnp.concatenate([wr, wi], axis=0)
    mb = _block(wr, -wi)
    mbi = _block(wr, wi)
    tr, ti = _cis(a[:, None] * a[None, :], N)
    tw = jnp.stack([jnp.tile(tr, (1, GROUP)), jnp.tile(ti, (1, GROUP))])
    FB = Lh // FA
    t = jnp.arange(FB // TILE, dtype=i32)[:, None, None, None]
    k1 = jnp.arange(FA, dtype=i32)[None, :, None, None]
    n1 = jnp.arange(FA, dtype=i32)[None, None, :, None]
    j = jnp.arange(TILE, dtype=i32)[None, None, None, :]
    vr, vi = _cis(k1 * (FB * n1 + TILE * t + j), Lh)
    eye = jnp.eye(TILE, dtype=F32)
    expand = lambda v: jnp.einsum("tknj,jm->tkjnm", v, eye).reshape(FB // TILE, FA * TILE, FA * TILE)
    fa = _block(expand(vr), expand(vi))
    b = jnp.arange(FB, dtype=i32)
    br, bi = _cis(b[:, None] * b[None, :], FB)
    fb = jnp.concatenate([br, -bi], axis=1)
    perm = jnp.eye(FA * TILE, dtype=F32).reshape(FA, TILE, FA * TILE).swapaxes(0, 1).reshape(FA * TILE, FA * TILE)
    cast = lambda t: t.astype(BF16)
    return dict(f1=cast(f1), i1=cast(i1), fk=cast(fk), mb=cast(mb), mbi=cast(mbi), tw=cast(tw),
                fa=cast(fa), fb=cast(fb), perm=cast(perm))


def _channel_dft(gd):
    a = jnp.arange(gd, dtype=jnp.int32)
    cr, ci = _cis(a[:, None] * a[None, :], gd)
    return jnp.concatenate([cr, ci], axis=1).astype(BF16)


def _inproj_nat_kernel(x_ref, gpre_ref, wf_ref, whg_ref, wm_ref, bm_ref, cd_ref,
                       z_ref, sfg_ref, shg_ref, gf_ref, gh_ref, *, C, D):
    xx = x_ref[0]
    ms = jnp.mean(xx * xx, axis=-1, keepdims=True)
    xm = (xx * lax.rsqrt(ms + RMS_EPS) * gpre_ref[...]).astype(BF16)
    shg_ref[0] = _silu(_dot(xm, whg_ref[...])).astype(BF16)
    pf = _dot(xm, wf_ref[...])
    sfg_ref[0] = _silu(pf[:, C:]).astype(BF16)
    fv = pf[:, :C].astype(BF16)
    gd = C // F_GROUPS
    for gi in range(F_GROUPS):
        zz = _dot(fv[:, gi * gd:(gi + 1) * gd], cd_ref[...])
        z_ref[0, 0, :, gi * gd:(gi + 1) * gd] = zz[:, :gd].astype(BF16)
        z_ref[0, 1, :, gi * gd:(gi + 1) * gd] = zz[:, gd:].astype(BF16)
    gate = jax.nn.sigmoid(_dot(xm, wm_ref[...]) + bm_ref[...])
    gf_ref[0] = gate[:, :D].astype(BF16)
    gh_ref[0] = gate[:, D:].astype(BF16)


def _inproj_nat(x, g_pre, w, b_merge, cd, *, tm):
    B, L, D = x.shape
    C = D
    const = lambda b, i: (0, 0)
    single = dict(pipeline_mode=pl.Buffered(1))
    row = pl.BlockSpec((1, tm, C), lambda b, i: (b, i, 0))
    act = jax.ShapeDtypeStruct((B, L, C), BF16)
    vmem = (2 * _nbytes((tm, D), F32) + 2 * 6 * _nbytes((tm, C), BF16)
            + 5 * _nbytes((D, C), BF16) + 5 * _nbytes((tm, 2 * C), F32))
    hg_col = w.shape[1] // C - 3
    last = w.shape[1] // (2 * C) - 1
    return pl.pallas_call(
        functools.partial(_inproj_nat_kernel, C=C, D=D),
        grid=(B, L // tm),
        in_specs=[pl.BlockSpec((1, tm, D), lambda b, i: (b, i, 0)),
                  pl.BlockSpec((1, D), const),
                  pl.BlockSpec((D, 2 * C), const, **single),
                  pl.BlockSpec((D, C), lambda b, i: (0, hg_col), **single),
                  pl.BlockSpec((D, 2 * D), lambda b, i: (0, last), **single),
                  pl.BlockSpec((1, 2 * D), const),
                  pl.BlockSpec(cd.shape, const)],
        out_specs=[pl.BlockSpec((1, 2, tm, C), lambda b, i: (b, 0, i, 0)), row, row, row, row],
        out_shape=[jax.ShapeDtypeStruct((B, 2, L, C), BF16), act, act, act, act],
        compiler_params=_params(vmem, ("parallel", "parallel")),
        name="inproj_nat",
    )(x, g_pre.reshape(1, D), w, w, w, b_merge.reshape(1, 2 * D), cd)


def _inproj_cm_kernel(x_ref, xp_ref, xn_ref, gpre_ref, w_ref, wsh_ref, o_ref, xe_ref, *pe_refs,
                      tp, tch):
    i = pl.program_id(1)
    ch = pl.program_id(2)
    last = pl.num_programs(1) - 1
    gain = gpre_ref[...]

    def norm(xx):
        ms = jnp.mean(xx * xx, axis=-1, keepdims=True)
        return xx * lax.rsqrt(ms + RMS_EPS) * gain

    @pl.when(ch == 0)
    def _():
        for s in range(2):
            xe_ref[s, :HALO] = (norm(xp_ref[0, s]) * (i > 0).astype(F32)).astype(BF16)
            xe_ref[s, HALO:HALO + tp] = norm(x_ref[0, s]).astype(BF16)
            xe_ref[s, HALO + tp:] = (norm(xn_ref[0, s]) * (i < last).astype(F32)).astype(BF16)

    nh = tch // MXU_COLS
    for s in range(2):
        for h in range(nh):
            cols = slice(h * MXU_COLS, (h + 1) * MXU_COLS)
            taps = wsh_ref[:, cols]
            pe_ref = pe_refs[s * nh + h]
            pe_ref[...] = _dot(xe_ref[s], w_ref[:, cols])
            conv = (pe_ref[pl.ds(HALO - 1, tp), :] * taps[0:1]
                    + pe_ref[pl.ds(HALO, tp), :] * taps[1:2]
                    + pe_ref[pl.ds(HALO + 1, tp), :] * taps[2:3])
            y = conv.T
            for q in range(tp // LANES):
                o_ref[0, 0, s, 0, pl.ds(h * MXU_COLS * SUB + q, MXU_COLS, stride=SUB), :] = (
                    y[:, q * LANES:(q + 1) * LANES])


def _inproj_cm(x, g_pre, w, w_short, *, tch):
    B, L, D = x.shape
    C = D
    P = B // 2
    tp = SUB * LANES
    NB = L // tp
    n_out = w_short.shape[1] // C
    per = C // tch
    first = 2 * C // tch
    nh = tp // HALO
    te = tp + 2 * HALO
    xv = x.reshape(P, 2, L, D)
    vmem = (2 * _nbytes((2, tp, D), F32) + _nbytes((2, te, D), BF16) + _nbytes((2, te, tch), F32)
            + 2 * _nbytes((D, tch), BF16) + 2 * _nbytes((2, tch * SUB, LANES), F32)
            + 8 * _nbytes((tch, tp), F32))
    return pl.pallas_call(
        functools.partial(_inproj_cm_kernel, tp=tp, tch=tch),
        grid=(P, NB, n_out * per),
        in_specs=[pl.BlockSpec((1, 2, tp, D), lambda p, i, c: (p, 0, i, 0)),
                  pl.BlockSpec((1, 2, HALO, D), lambda p, i, c: (p, 0, jnp.maximum(i * nh - 1, 0), 0)),
                  pl.BlockSpec((1, 2, HALO, D),
                               lambda p, i, c: (p, 0, jnp.minimum((i + 1) * nh, L // HALO - 1), 0)),
                  pl.BlockSpec((1, D), lambda p, i, c: (0, 0)),
                  pl.BlockSpec((D, tch), lambda p, i, c: (0, first + c)),
                  pl.BlockSpec((3, tch), lambda p, i, c: (0, c))],
        out_specs=pl.BlockSpec((1, 1, 2, 1, tch * SUB, LANES),
                               lambda p, i, c: (c // per, p, 0, i, c % per, 0)),
        out_shape=jax.ShapeDtypeStruct((n_out, P, 2, NB, C * SUB, LANES), F32),
        scratch_shapes=[pltpu.VMEM((2, te, D), BF16)]
                       + [pltpu.VMEM((te, MXU_COLS), F32)] * (2 * tch // MXU_COLS),
        compiler_params=_params(vmem, ("parallel", "parallel", "arbitrary")),
        name="inproj_cm",
    )(xv, xv, xv, g_pre.reshape(1, D), w, w_short)


def _fourier_a_kernel(z_ref, fa_ref, a_ref, *, tt):
    for tl in range(tt):
        x = z_ref[0, :, :, tl]
        y = _dot(fa_ref[tl], x.reshape(-1, x.shape[-1]))
        a_ref[0, :, :, tl] = y.astype(BF16).reshape(x.shape)


def _fourier_a(z, fa, *, tt):
    B, _, _, T, _, C = z.shape
    blk = (1, 2, FA, tt, TILE, C)
    vmem = 4 * _nbytes(blk, BF16) + 2 * _nbytes((tt,) + fa.shape[1:], BF16) + 4 * _nbytes((2 * FA * TILE, C), F32)
    return pl.pallas_call(
        functools.partial(_fourier_a_kernel, tt=tt),
        grid=(T // tt, B),
        in_specs=[pl.BlockSpec(blk, lambda t, b: (b, 0, 0, t, 0, 0)),
                  pl.BlockSpec((tt,) + fa.shape[1:], lambda t, b: (t, 0, 0))],
        out_specs=pl.BlockSpec(blk, lambda t, b: (b, 0, 0, t, 0, 0)),
        out_shape=jax.ShapeDtypeStruct(z.shape, BF16),
        compiler_params=_params(vmem, ("parallel", "parallel")),
        name="fourier_a",
    )(z, fa)


def _fourier_b_kernel(a_ref, fb_ref, o_ref, *, scale, kb):
    for q in range(kb):
        a = a_ref[0, :, q]
        fz = _dot(fb_ref[...], a.reshape(-1, a.shape[-1])) * scale
        o_ref[0, q] = fz.astype(BF16)


def _fourier_b(a, fb, *, scale, kb=2):
    B, _, _, FB, C = a.shape
    vmem = 2 * 3 * kb * _nbytes((FB, C), BF16) + 2 * _nbytes(fb.shape, BF16) + 4 * _nbytes((FB, C), F32)
    return pl.pallas_call(
        functools.partial(_fourier_b_kernel, scale=scale, kb=kb),
        grid=(B, FA // kb),
        in_specs=[pl.BlockSpec((1, 2, kb, FB, C), lambda b, k: (b, 0, k, 0, 0)),
                  pl.BlockSpec(fb.shape, lambda b, k: (0, 0))],
        out_specs=pl.BlockSpec((1, kb, FB, C), lambda b, k: (b, k, 0, 0)),
        out_shape=jax.ShapeDtypeStruct((B, FA, FB, C), BF16),
        compiler_params=_params(vmem, ("parallel", "parallel")),
        name="fourier_b",
    )(a, fb)


def _gather_group(load, nb):
    rows = [jnp.concatenate([load(i, j) for j in range(GROUP)], axis=1) for i in range(nb)]
    return jnp.concatenate(rows, axis=0)


def _to_rows(ar, ai):
    R = ar.shape[0]
    parts = [jnp.concatenate([ar[:, j * R:(j + 1) * R], ai[:, j * R:(j + 1) * R]], axis=1)
             for j in range(GROUP)]
    return jnp.concatenate(parts, axis=0)


def _to_lanes(z):
    R = z.shape[1] // 2
    zr = jnp.concatenate([z[j * R:(j + 1) * R, :R] for j in range(GROUP)], axis=1)
    zi = jnp.concatenate([z[j * R:(j + 1) * R, R:] for j in range(GROUP)], axis=1)
    return zr, zi


def _forward(a, tr, ti, mb):
    R = a.shape[0] // 2
    ar, ai = a[:R].astype(BF16), a[R:].astype(BF16)
    return _dot(_to_rows(ar * tr - ai * ti, ar * ti + ai * tr), mb)


def _filter_gen_kernel(bands_ref, w1_ref, b1_ref, f1_ref, w2_ref, b2_ref, f2_ref, w3_ref,
                       dec_ref, k_ref, s_ref, *, tp, tch, L):
    i = pl.program_id(0)
    n = i * tp + lax.broadcasted_iota(jnp.int32, (1, tp), 1)
    t = jnp.where(n < L, n, 2 * L - 1 - n).astype(F32)
    tnorm = t / (L - 1)
    ang = (2.0 * math.pi * t) / L
    row = lax.broadcasted_iota(jnp.int32, (LANES, tp), 0)
    arg = bands_ref[...] * ang
    feats = jnp.where(row == 0, tnorm,
                      jnp.where(row <= FILT_BANDS, jnp.cos(arg),
                                jnp.where(row <= 2 * FILT_BANDS, -jnp.sin(arg), 0.0)))
    hdot = lambda a, b: jnp.dot(a, b, precision=HIGHEST, preferred_element_type=F32)
    h = jnp.sin(f1_ref[...] * (hdot(w1_ref[...], feats) + b1_ref[...]))
    hb = jnp.sin(f2_ref[...] * (hdot(w2_ref[...], h) + b2_ref[...])).astype(BF16)

    @pl.when(i == 0)
    def _():
        s_ref[...] = jnp.zeros_like(s_ref)

    for c in range(w3_ref.shape[0] // tch):
        rows = pl.ds(c * tch, tch)
        hc = _dot(w3_ref[rows, :].astype(BF16), hb) * jnp.exp(-tnorm * jnp.abs(dec_ref[rows, :]))
        s_ref[rows, :] += jnp.sum(jnp.abs(hc), axis=1, keepdims=True)
        hc = jnp.where(n == L, 0.0, hc)
        for q in range(tp // LANES):
            k_ref[0, pl.ds(c * tch * SUB + q, tch, stride=SUB), :] = hc[:, q * LANES:(q + 1) * LANES]


def _filter_gen(L, C, w1, b1, f1, w2, b2, f2, w3, decay, *, tch):
    N = 2 * L
    tp = SUB * LANES
    emb, hid = w1.shape
    W = 2 * C
    bands = jnp.linspace(1e-4, FILT_BANDS - 1, FILT_BANDS, dtype=F32)
    col = jnp.zeros((LANES, 1), F32).at[1:1 + FILT_BANDS, 0].set(bands)
    col = col.at[1 + FILT_BANDS:1 + 2 * FILT_BANDS, 0].set(bands)
    w1t = jnp.zeros((hid, LANES), F32).at[:, :emb].set(w1.T)
    per_dir = L // tp
    const = lambda i: (0, 0)
    vmem = (2 * _nbytes((W * SUB, LANES), F32) + 2 * _nbytes((W, LANES), F32) * 3
            + 6 * _nbytes((tch, tp), F32) + 8 * _nbytes((LANES, tp), F32))
    return pl.pallas_call(
        functools.partial(_filter_gen_kernel, tp=tp, tch=tch, L=L),
        grid=(N // tp,),
        in_specs=[pl.BlockSpec((LANES, 1), const),
                  pl.BlockSpec((hid, LANES), const), pl.BlockSpec((hid, 1), const),
                  pl.BlockSpec((hid, 1), const),
                  pl.BlockSpec((hid, hid), const), pl.BlockSpec((hid, 1), const),
                  pl.BlockSpec((hid, 1), const),
                  pl.BlockSpec((W, hid), lambda i: (i // per_dir, 0)),
                  pl.BlockSpec((W, 1), lambda i: (i // per_dir, 0))],
        out_specs=[pl.BlockSpec((1, W * SUB, LANES), lambda i: (i, 0, 0)),
                   pl.BlockSpec((W, 1), const)],
        out_shape=[jax.ShapeDtypeStruct((N // tp, W * SUB, LANES), F32),
                   jax.ShapeDtypeStruct((W, 1), F32)],
        compiler_params=_params(vmem, ("arbitrary",)),
        name="filter_gen",
    )(col, w1t, b1.reshape(hid, 1), f1.reshape(hid, 1), w2.T, b2.reshape(hid, 1),
      f2.reshape(hid, 1), w3.T, decay.reshape(2 * W, 1))


def _filter_spec_kernel(k_ref, s_ref, fk_ref, tw_ref, mb_ref, kf_ref, *, tcf, R):
    nb = k_ref.shape[0]

    def body(g, carry):
        r0 = pl.multiple_of(g * GROUP * SUB, GROUP * SUB)
        x = _gather_group(lambda i, j: k_ref[i, pl.ds(r0 + j * SUB, SUB), :], nb)
        a = _dot(fk_ref[...], x.astype(BF16))
        z = _forward(a, tw_ref[0], tw_ref[1], mb_ref[...])
        c0 = pl.program_id(0) * tcf + g * GROUP
        inv = jnp.concatenate(
            [jnp.broadcast_to(1.0 / s_ref[pl.ds(c0 + j, 1), :], (R, 1)) for j in range(GROUP)], axis=0)
        kf_ref[pl.ds(pl.multiple_of(g * GROUP * R, GROUP * R), GROUP * R), :] = (z * inv).astype(BF16)
        return carry

    lax.fori_loop(0, tcf // GROUP, body, 0)


def _filter_spec(k, s, tabs, *, tcf):
    nb, rows, _ = k.shape
    W = rows // SUB
    R = LANES
    vmem = (2 * _nbytes((nb, tcf * SUB, LANES), F32) + 2 * _nbytes((tcf * R, 2 * R), BF16)
            + 2 * _nbytes((W, LANES), F32) + 16 * _nbytes((2 * R, GROUP * R), F32))
    return pl.pallas_call(
        functools.partial(_filter_spec_kernel, tcf=tcf, R=R),
        grid=(W // tcf,),
        in_specs=[pl.BlockSpec((nb, tcf * SUB, LANES), lambda c: (0, c, 0)),
                  pl.BlockSpec((W, 1), lambda c: (0, 0)),
                  pl.BlockSpec(tabs["fk"].shape, lambda c: (0, 0)),
                  pl.BlockSpec(tabs["tw"].shape, lambda c: (0, 0, 0)),
                  pl.BlockSpec(tabs["mb"].shape, lambda c: (0, 0))],
        out_specs=pl.BlockSpec((tcf * R, 2 * R), lambda c: (c, 0)),
        out_shape=jax.ShapeDtypeStruct((W * R, 2 * R), BF16),
        compiler_params=_params(vmem, ("parallel",)),
        name="filter_spec",
    )(k, s, tabs["fk"], tabs["tw"], tabs["mb"])


def _hyena_kernel(v_ref, x1_ref, x2_ref, kf0_ref, kf1_ref, d_ref, f1_ref, i1_ref,
                  tw_ref, mb_ref, mbi_ref, o_ref, *, tcc, R):
    cb = pl.program_id(0)
    nb = v_ref.shape[3]
    H = R // 2

    def conv(x, kf):
        tr, ti = tw_ref[0], tw_ref[1]
        a = _dot(f1_ref[...], x.astype(BF16))
        z = _forward(a, tr, ti, mb_ref[...]).astype(BF16)
        zr, zi = z[:, :R], z[:, R:]
        kr, ki = kf[:, :R], kf[:, R:]
        y = jnp.concatenate([zr * kr - zi * ki, zr * ki + zi * kr], axis=1)
        br, bi = _to_lanes(_dot(y, mbi_ref[...]).astype(BF16))
        b = jnp.concatenate([br * tr + bi * ti, bi * tr - br * ti], axis=0)
        return _dot(i1_ref[...], b)

    def body(g, carry):
        r0 = pl.multiple_of(g * GROUP * SUB, GROUP * SUB)
        k0 = pl.multiple_of(g * GROUP * R, GROUP * R)

        def load(ref):
            halves = [_gather_group(lambda i, j, s=s: ref[0, 0, s, i, pl.ds(r0 + j * SUB, SUB), :], nb)
                      for s in range(2)]
            return jnp.concatenate(halves, axis=0)

        drow = cb * (tcc // GROUP) + g
        v = load(v_ref)
        z1 = load(x1_ref) * (conv(v, kf0_ref[pl.ds(k0, GROUP * R), :]) + d_ref[0, pl.ds(drow, 1), :] * v)
        u = load(x2_ref) * (conv(z1, kf1_ref[pl.ds(k0, GROUP * R), :]) + d_ref[1, pl.ds(drow, 1), :] * z1)
        for s in range(2):
            for i in range(nb):
                for j in range(GROUP):
                    o_ref[0, s, i, pl.ds(r0 + j * SUB, SUB), :] = (
                        u[s * H + i * SUB:s * H + (i + 1) * SUB, j * R:(j + 1) * R])
        return carry

    lax.fori_loop(0, tcc // GROUP, body, 0, unroll=4)


def _hyena(acts, kf, dl, tabs, *, tcc):
    _, P, _, NB, rows, _ = acts.shape
    C = rows // SUB
    R = LANES
    per = C // tcc
    act = lambda w: pl.BlockSpec((1, 1, 2, NB, tcc * SUB, LANES), lambda c, p, w=w: (w, p, 0, 0, c, 0))
    const2 = lambda c, p: (0, 0)
    const3 = lambda c, p: (0, 0, 0)
    vmem = (2 * 4 * _nbytes((2, NB, tcc * SUB, LANES), F32) + 2 * 2 * _nbytes((tcc * R, 2 * R), BF16)
            + 2 * _nbytes(dl.shape, F32) + 2 * _nbytes(tabs["tw"].shape, BF16)
            + 24 * _nbytes((2 * R, GROUP * R), F32))
    return pl.pallas_call(
        functools.partial(_hyena_kernel, tcc=tcc, R=R),
        grid=(per, P),
        in_specs=[act(0), act(1), act(2),
                  pl.BlockSpec((tcc * R, 2 * R), lambda c, p: (c, 0)),
                  pl.BlockSpec((tcc * R, 2 * R), lambda c, p: (per + c, 0)),
                  pl.BlockSpec(dl.shape, const3),
                  pl.BlockSpec(tabs["f1"].shape, const2),
                  pl.BlockSpec(tabs["i1"].shape, const2),
                  pl.BlockSpec(tabs["tw"].shape, const3),
                  pl.BlockSpec(tabs["mb"].shape, const2),
                  pl.BlockSpec(tabs["mbi"].shape, const2)],
        out_specs=pl.BlockSpec((1, 2, NB, tcc * SUB, LANES), lambda c, p: (p, 0, 0, c, 0)),
        out_shape=jax.ShapeDtypeStruct((P, 2, NB, rows, LANES), F32),
        compiler_params=_params(vmem, ("parallel", "parallel")),
        name="hyena",
    )(acts, acts, acts, kf, kf, dl, tabs["f1"], tabs["i1"], tabs["tw"], tabs["mb"], tabs["mbi"])


def _merge_kernel(u_ref, fz_ref, sfg_ref, shg_ref, gf_ref, gh_ref, x_ref, perm_ref, wfo_ref, who_ref,
                  wout_ref, gpost_ref, o_ref, *, C, D, qq):
    q0 = pl.program_id(2) * qq
    tp = qq * LANES
    nat = lambda ref: ref[0].reshape(2 * tp, ref.shape[-1])
    us = [u_ref[0, s, 0, pl.ds(q0 + q, C, stride=SUB), :].T for s in range(2) for q in range(qq)]
    u = jnp.concatenate(us, axis=0) * nat(shg_ref).astype(F32)
    yh = _dot(u.astype(BF16), who_ref[...])
    fz = jnp.concatenate([_dot(perm_ref[...], fz_ref[0, s].reshape(tp, C)) for s in range(2)], axis=0)
    yf = _dot((fz * nat(sfg_ref).astype(F32)).astype(BF16), wfo_ref[...])
    m = nat(gf_ref).astype(F32) * yf + nat(gh_ref).astype(F32) * yh
    out = _dot(m.astype(BF16), wout_ref[...])
    var = jnp.mean(out * out, axis=-1, keepdims=True)
    res = nat(x_ref) + out * lax.rsqrt(var + RMS_EPS) * gpost_ref[...]
    o_ref[0] = res.reshape(2, tp, D)


def _merge(u, fz, sfg, shg, gf, gh, x, perm, wfo, who, wout, g_post):
    P, _, NB, rows, _ = u.shape
    C = rows // SUB
    B, L, D = x.shape
    tp = FA * TILE
    qq = tp // LANES
    pv = lambda t: t.reshape(P, 2, L, t.shape[-1])
    nat = lambda w: pl.BlockSpec((1, 2, tp, w), lambda p, i, q: (p, 0, i * (SUB // qq) + q, 0))
    const = lambda p, i, q: (0, 0)
    single = dict(pipeline_mode=pl.Buffered(1))
    weights = [perm, wfo, who, wout]
    vmem = (2 * _nbytes((2, rows, LANES), F32) + 2 * 5 * _nbytes((2, tp, D), BF16)
            + 4 * _nbytes((2, tp, D), F32) + sum(_nbytes(w.shape, BF16) for w in weights)
            + 8 * _nbytes((2 * tp, D), F32))
    out = pl.pallas_call(
        functools.partial(_merge_kernel, C=C, D=D, qq=qq),
        grid=(P, NB, SUB // qq),
        in_specs=[pl.BlockSpec((1, 2, 1, rows, LANES), lambda p, i, q: (p, 0, i, 0, 0)),
                  pl.BlockSpec((1, 2, FA, TILE, C), lambda p, i, q: (p, 0, 0, i * (SUB // qq) + q, 0)),
                  nat(C), nat(C), nat(D), nat(D), nat(D)]
                 + [pl.BlockSpec(w.shape, const, **single) for w in weights]
                 + [pl.BlockSpec((1, D), const)],
        out_specs=nat(D),
        out_shape=jax.ShapeDtypeStruct((P, 2, L, D), F32),
        compiler_params=_params(vmem, ("parallel", "parallel", "arbitrary")),
        name="merge",
    )(u, fz.reshape((P, 2) + fz.shape[1:]), pv(sfg), pv(shg), pv(gf), pv(gh), pv(x), *weights,
      g_post.reshape(1, D))
    return out.reshape(B, L, D)


def _tile(n, want):
    t = min(n, want)
    while n % t:
        t //= 2
    return t


def _hyena_filters(L, C, tabs, w1, b1, f1, w2, b2, f2, w3, decay):
    k, s = _filter_gen(L, C, w1, b1, f1, w2, b2, f2, w3, decay, tch=_tile(2 * C, 512))
    return _filter_spec(k, s, tabs, tcf=_tile(2 * C, 32))


def _encoder_layer(x, kf, tabs, cd, dl, g_pre, w, w_short, b_merge, wfo, who, wout, g_post):
    B, L, D = x.shape
    C = D
    FB = L // FA
    z, sfg, shg, gf, gh = _inproj_nat(x, g_pre, w, b_merge, cd, tm=_tile(L, 512))
    acts = _inproj_cm(x, g_pre, w, w_short, tch=_tile(C, 1024))

    af = _fourier_a(z.reshape(B, 2, FA, FB // TILE, TILE, C), tabs["fa"], tt=4)
    scale = 1.0 / math.sqrt(L * (C // F_GROUPS))
    fz = _fourier_b(af.reshape(B, 2, FA, FB, C), tabs["fb"], scale=scale)

    u = _hyena(acts, kf, dl, tabs, tcc=_tile(C, 32))
    return _merge(u, fz, sfg, shg, gf, gh, x, tabs["perm"], wfo, who, wout, g_post)


def kernel(x_prompt, x_sample, g_pre, w_in, w_short, filt_w1, filt_b1, filt_freq1, filt_w2,
           filt_b2, filt_freq2, filt_w3, filt_decay, hyena_d, w_fourier_out, w_hyena_out,
           b_merge, w_out, g_post):
    depth = g_pre.shape[0]
    L, D = x_prompt.shape[1], x_prompt.shape[2]
    assert x_sample.shape[1:] == (L, D)
    C = D
    R = LANES
    assert R * R == 2 * L and C % (GROUP * F_GROUPS) == 0
    assert x_prompt.shape[0] % 2 == 0 and x_sample.shape[0] % 2 == 0
    tabs = _tables(R)
    cd = _channel_dft(C // F_GROUPS)
    ys = [x_prompt, x_sample]
    for i in range(depth):
        w = w_in[i].astype(BF16)
        dl = jnp.repeat(hyena_d[i], R, axis=-1).reshape(2, C // GROUP, GROUP * R)
        kf = _hyena_filters(L, C, tabs, filt_w1[i], filt_b1[i], filt_freq1[i], filt_w2[i],
                            filt_b2[i], filt_freq2[i], filt_w3[i], filt_decay[i])
        args = (kf, tabs, cd, dl, g_pre[i], w, w_short[i], b_merge[i],
                w_fourier_out[i].astype(BF16), w_hyena_out[i].astype(BF16),
                w_out[i].astype(BF16), g_post[i])
        ys = [_encoder_layer(y, *args) for y in ys]
    return tuple(ys)
```

```python
import functools
import math

import jax
import jax.numpy as jnp
from jax import lax
from jax.experimental import pallas as pl
from jax.experimental.pallas import tpu as pltpu

BF16 = jnp.bfloat16
F32 = jnp.float32

F_GROUPS = 4
RMS_EPS = 1e-6
FILT_BANDS = 16
TILE = 16
HALO = TILE
FA = 16
LANES = 128
SUB = 8
MXU_COLS = 256
GROUP = 32
V7X_VMEM_BYTES = 64 * 1024 * 1024
VMEM_CAP = V7X_VMEM_BYTES - 8 * 1024 * 1024
HIGHEST = lax.Precision.HIGHEST


def _dot(a, b):
    return jnp.dot(a, b, preferred_element_type=F32)


def _params(vmem_bytes, semantics):
    limit = min(VMEM_CAP, vmem_bytes + 8 * 1024 * 1024)
    return pltpu.CompilerParams(dimension_semantics=semantics, vmem_limit_bytes=int(limit))


def _nbytes(shape, dtype):
    return math.prod(shape) * jnp.dtype(dtype).itemsize


def _silu(x):
    return x * jax.nn.sigmoid(x)


def _cis(num, den):
    ang = (num % den).astype(F32) * (2.0 * math.pi / den)
    return jnp.cos(ang), -jnp.sin(ang)


def _block(rr, ri):
    top = jnp.concatenate([rr, -ri], axis=-1)
    bot = jnp.concatenate([ri, rr], axis=-1)
    return jnp.concatenate([top, bot], axis=-2)


def _tables(R):
    N = R * R
    Lh = N // 2
    H = R // 2
    i32 = jnp.int32
    a = jnp.arange(R, dtype=i32)
    wr, wi = _cis(a[:, None] * a[None, :], R)
    f1 = _block(wr[:, :H], wi[:, :H])
    i1 = _block(wr[:, :H].T, -wi[:, :H].T) * (1.0 / N)
    fk = jnp.concatenate([wr, wi], axis=0)
    mb = _block(wr, -wi)
    mbi = _block(wr, wi)
    tr, ti = _cis(a[:, None] * a[None, :], N)
    tw = jnp.stack([jnp.tile(tr, (1, GROUP)), jnp.tile(ti, (1, GROUP))])
    FB = Lh // FA
    t = jnp.arange(FB // TILE, dtype=i32)[:, None, None, None]
    k1 = jnp.arange(FA, dtype=i32)[None, :, None, None]
    n1 = jnp.arange(FA, dtype=i32)[None, None, :, None]
    j = jnp.arange(TILE, dtype=i32)[None, None, None, :]
    vr, vi = _cis(k1 * (FB * n1 + TILE * t + j), Lh)
    eye = jnp.eye(TILE, dtype=F32)
    expand = lambda v: jnp.einsum("tknj,jm->tkjnm", v, eye).reshape(FB // TILE, FA * TILE, FA * TILE)
    fa = _block(expand(vr), expand(vi))
    b = jnp.arange(FB, dtype=i32)
    br, bi = _cis(b[:, None] * b[None, :], FB)
    fb = jnp.concatenate([br, -bi], axis=1)
    perm = jnp.eye(FA * TILE, dtype=F32).reshape(FA, TILE, FA * TILE).swapaxes(0, 1).reshape(FA * TILE, FA * TILE)
    cast = lambda t: t.astype(BF16)
    return dict(f1=cast(f1), i1=cast(i1), fk=cast(fk), mb=cast(mb), mbi=cast(mbi), tw=cast(tw),
                fa=cast(fa), fb=cast(fb), perm=cast(perm))


def _channel_dft(gd):
    a = jnp.arange(gd, dtype=jnp.int32)
    cr, ci = _cis(a[:, None] * a[None, :], gd)
    return jnp.concatenate([cr, ci], axis=1).astype(BF16)


def _inproj_nat_kernel(x_ref, gpre_ref, w_ref, bm_ref, cd_ref,
                       z_ref, sfg_ref, shg_ref, gf_ref, gh_ref, *, C, D):
    xx = x_ref[0]
    ms = jnp.mean(xx * xx, axis=-1, keepdims=True)
    xm = (xx * lax.rsqrt(ms + RMS_EPS) * gpre_ref[...]).astype(BF16)
    wf_ref, whg_ref, wm_ref = w_ref.at[:, :2 * C], w_ref.at[:, 2 * C:3 * C], w_ref.at[:, 3 * C:]
    shg_ref[0] = _silu(_dot(xm, whg_ref[...])).astype(BF16)
    pf = _dot(xm, wf_ref[...])
    sfg_ref[0] = _silu(pf[:, C:]).astype(BF16)
    fv = pf[:, :C].astype(BF16)
    gd = C // F_GROUPS
    for gi in range(F_GROUPS):
        zz = _dot(fv[:, gi * gd:(gi + 1) * gd], cd_ref[...])
        z_ref[0, 0, :, gi * gd:(gi + 1) * gd] = zz[:, :gd].astype(BF16)
        z_ref[0, 1, :, gi * gd:(gi + 1) * gd] = zz[:, gd:].astype(BF16)
    gate = jax.nn.sigmoid(_dot(xm, wm_ref[...]) + bm_ref[...])
    gf_ref[0] = gate[:, :D].astype(BF16)
    gh_ref[0] = gate[:, D:].astype(BF16)


def _inproj_nat(x, g_pre, w, b_merge, cd, *, tm):
    B, L, D = x.shape
    C = D
    const = lambda b, i: (0, 0)
    row = pl.BlockSpec((1, tm, C), lambda b, i: (b, i, 0))
    act = jax.ShapeDtypeStruct((B, L, C), BF16)
    vmem = (2 * _nbytes((tm, D), F32) + 2 * 6 * _nbytes((tm, C), BF16)
            + _nbytes(w.shape, BF16) + 5 * _nbytes((tm, 2 * C), F32))
    return pl.pallas_call(
        functools.partial(_inproj_nat_kernel, C=C, D=D),
        grid=(B, L // tm),
        in_specs=[pl.BlockSpec((1, tm, D), lambda b, i: (b, i, 0)),
                  pl.BlockSpec((1, D), const),
                  pl.BlockSpec(w.shape, const, pipeline_mode=pl.Buffered(1)),
                  pl.BlockSpec((1, 2 * D), const),
                  pl.BlockSpec(cd.shape, const)],
        out_specs=[pl.BlockSpec((1, 2, tm, C), lambda b, i: (b, 0, i, 0)), row, row, row, row],
        out_shape=[jax.ShapeDtypeStruct((B, 2, L, C), BF16), act, act, act, act],
        compiler_params=_params(vmem, ("parallel", "parallel")),
        name="inproj_nat",
    )(x, g_pre.reshape(1, D), w, b_merge.reshape(1, 2 * D), cd)


def _inproj_cm_kernel(x_ref, xp_ref, xn_ref, gpre_ref, w_ref, wsh_ref, o_ref, xe_ref, *pe_refs,
                      tp, tch):
    i = pl.program_id(1)
    ch = pl.program_id(2)
    last = pl.num_programs(1) - 1
    gain = gpre_ref[...]

    def norm(xx):
        ms = jnp.mean(xx * xx, axis=-1, keepdims=True)
        return xx * lax.rsqrt(ms + RMS_EPS) * gain

    @pl.when(ch == 0)
    def _():
        for s in range(2):
            xe_ref[s, :HALO] = (norm(xp_ref[0, s]) * (i > 0).astype(F32)).astype(BF16)
            xe_ref[s, HALO:HALO + tp] = norm(x_ref[0, s]).astype(BF16)
            xe_ref[s, HALO + tp:] = (norm(xn_ref[0, s]) * (i < last).astype(F32)).astype(BF16)

    nh = tch // MXU_COLS
    for s in range(2):
        for h in range(nh):
            cols = slice(h * MXU_COLS, (h + 1) * MXU_COLS)
            taps = wsh_ref[:, cols]
            pe_ref = pe_refs[s * nh + h]
            pe_ref[...] = _dot(xe_ref[s], w_ref[:, cols])
            conv = (pe_ref[pl.ds(HALO - 1, tp), :] * taps[0:1]
                    + pe_ref[pl.ds(HALO, tp), :] * taps[1:2]
                    + pe_ref[pl.ds(HALO + 1, tp), :] * taps[2:3])
            y = conv.T
            for q in range(tp // LANES):
                o_ref[0, 0, s, 0, pl.ds(h * MXU_COLS * SUB + q, MXU_COLS, stride=SUB), :] = (
                    y[:, q * LANES:(q + 1) * LANES])


def _inproj_cm(x, g_pre, w, w_short, *, tch):
    B, L, D = x.shape
    C = D
    P = B // 2
    tp = SUB * LANES
    NB = L // tp
    n_out = w_short.shape[1] // C
    per = C // tch
    nh = tp // HALO
    te = tp + 2 * HALO
    xv = x.reshape(P, 2, L, D)
    vmem = (2 * _nbytes((2, tp, D), F32) + _nbytes((2, te, D), BF16) + _nbytes((2, te, tch), F32)
            + 2 * _nbytes((D, tch), BF16) + 2 * _nbytes((2, tch * SUB, LANES), F32)
            + 8 * _nbytes((tch, tp), F32))
    return pl.pallas_call(
        functools.partial(_inproj_cm_kernel, tp=tp, tch=tch),
        grid=(P, NB, n_out * per),
        in_specs=[pl.BlockSpec((1, 2, tp, D), lambda p, i, c: (p, 0, i, 0)),
                  pl.BlockSpec((1, 2, HALO, D), lambda p, i, c: (p, 0, jnp.maximum(i * nh - 1, 0), 0)),
                  pl.BlockSpec((1, 2, HALO, D),
                               lambda p, i, c: (p, 0, jnp.minimum((i + 1) * nh, L // HALO - 1), 0)),
                  pl.BlockSpec((1, D), lambda p, i, c: (0, 0)),
                  pl.BlockSpec((D, tch), lambda p, i, c: (0, c)),
                  pl.BlockSpec((3, tch), lambda p, i, c: (0, c))],
        out_specs=pl.BlockSpec((1, 1, 2, 1, tch * SUB, LANES),
                               lambda p, i, c: (c // per, p, 0, i, c % per, 0)),
        out_shape=jax.ShapeDtypeStruct((n_out, P, 2, NB, C * SUB, LANES), F32),
        scratch_shapes=[pltpu.VMEM((2, te, D), BF16)]
                       + [pltpu.VMEM((te, MXU_COLS), F32)] * (2 * tch // MXU_COLS),
        compiler_params=_params(vmem, ("parallel", "parallel", "arbitrary")),
        name="inproj_cm",
    )(xv, xv, xv, g_pre.reshape(1, D), w, w_short)


def _fourier_a_kernel(z_ref, fa_ref, a_ref, *, tt):
    for tl in range(tt):
        x = z_ref[0, :, :, tl]
        y = _dot(fa_ref[tl], x.reshape(-1, x.shape[-1]))
        a_ref[0, :, :, tl] = y.astype(BF16).reshape(x.shape)


def _fourier_a(z, fa, *, tt):
    B, _, _, T, _, C = z.shape
    blk = (1, 2, FA, tt, TILE, C)
    vmem = 4 * _nbytes(blk, BF16) + 2 * _nbytes((tt,) + fa.shape[1:], BF16) + 4 * _nbytes((2 * FA * TILE, C), F32)
    return pl.pallas_call(
        functools.partial(_fourier_a_kernel, tt=tt),
        grid=(T // tt, B),
        in_specs=[pl.BlockSpec(blk, lambda t, b: (b, 0, 0, t, 0, 0)),
                  pl.BlockSpec((tt,) + fa.shape[1:], lambda t, b: (t, 0, 0))],
        out_specs=pl.BlockSpec(blk, lambda t, b: (b, 0, 0, t, 0, 0)),
        out_shape=jax.ShapeDtypeStruct(z.shape, BF16),
        compiler_params=_params(vmem, ("parallel", "parallel")),
        name="fourier_a",
    )(z, fa)


def _fourier_b_kernel(a_ref, fb_ref, o_ref, *, scale, kb):
    for q in range(kb):
        a = a_ref[0, :, q]
        fz = _dot(fb_ref[...], a.reshape(-1, a.shape[-1])) * scale
        o_ref[0, q] = fz.astype(BF16)


def _fourier_b(a, fb, *, scale, kb=2):
    B, _, _, FB, C = a.shape
    vmem = 2 * 3 * kb * _nbytes((FB, C), BF16) + 2 * _nbytes(fb.shape, BF16) + 4 * _nbytes((FB, C), F32)
    return pl.pallas_call(
        functools.partial(_fourier_b_kernel, scale=scale, kb=kb),
        grid=(B, FA // kb),
        in_specs=[pl.BlockSpec((1, 2, kb, FB, C), lambda b, k: (b, 0, k, 0, 0)),
                  pl.BlockSpec(fb.shape, lambda b, k: (0, 0))],
        out_specs=pl.BlockSpec((1, kb, FB, C), lambda b, k: (b, k, 0, 0)),
        out_shape=jax.ShapeDtypeStruct((B, FA, FB, C), BF16),
        compiler_params=_params(vmem, ("parallel", "parallel")),
        name="fourier_b",
    )(a, fb)


def _gather_group(load, nb):
    rows = [jnp.concatenate([load(i, j) for j in range(GROUP)], axis=1) for i in range(nb)]
    return jnp.concatenate(rows, axis=0)


def _to_rows(ar, ai):
    R = ar.shape[0]
    parts = [jnp.concatenate([ar[:, j * R:(j + 1) * R], ai[:, j * R:(j + 1) * R]], axis=1)
             for j in range(GROUP)]
    return jnp.concatenate(parts, axis=0)


def _to_lanes(z):
    R = z.shape[1] // 2
    zr = jnp.concatenate([z[j * R:(j + 1) * R, :R] for j in range(GROUP)], axis=1)
    zi = jnp.concatenate([z[j * R:(j + 1) * R, R:] for j in range(GROUP)], axis=1)
    return zr, zi


def _forward(a, tr, ti, mb):
    R = a.shape[0] // 2
    ar, ai = a[:R].astype(BF16), a[R:].astype(BF16)
    return _dot(_to_rows(ar * tr - ai * ti, ar * ti + ai * tr), mb)


def _filter_gen_kernel(bands_ref, w1_ref, b1_ref, f1_ref, w2_ref, b2_ref, f2_ref, w3_ref,
                       dec_ref, k_ref, s_ref, *, tp, tch, L):
    i = pl.program_id(0)
    n = i * tp + lax.broadcasted_iota(jnp.int32, (1, tp), 1)
    t = jnp.where(n < L, n, 2 * L - 1 - n).astype(F32)
    tnorm = t / (L - 1)
    ang = (2.0 * math.pi * t) / L
    row = lax.broadcasted_iota(jnp.int32, (LANES, tp), 0)
    arg = bands_ref[...] * ang
    feats = jnp.where(row == 0, tnorm,
                      jnp.where(row <= FILT_BANDS, jnp.cos(arg),
                                jnp.where(row <= 2 * FILT_BANDS, -jnp.sin(arg), 0.0)))
    hdot = lambda a, b: jnp.dot(a, b, precision=HIGHEST, preferred_element_type=F32)
    h = jnp.sin(f1_ref[...] * (hdot(w1_ref[...], feats) + b1_ref[...]))
    hb = jnp.sin(f2_ref[...] * (hdot(w2_ref[...], h) + b2_ref[...])).astype(BF16)

    @pl.when(i == 0)
    def _():
        s_ref[...] = jnp.zeros_like(s_ref)

    for c in range(w3_ref.shape[0] // tch):
        rows = pl.ds(c * tch, tch)
        hc = _dot(w3_ref[rows, :].astype(BF16), hb) * jnp.exp(-tnorm * jnp.abs(dec_ref[rows, :]))
        s_ref[rows, :] += jnp.sum(jnp.abs(hc), axis=1, keepdims=True)
        hc = jnp.where(n == L, 0.0, hc)
        for q in range(tp // LANES):
            k_ref[0, pl.ds(c * tch * SUB + q, tch, stride=SUB), :] = hc[:, q * LANES:(q + 1) * LANES]


def _filter_gen(L, C, w1, b1, f1, w2, b2, f2, w3, decay, *, tch):
    N = 2 * L
    tp = SUB * LANES
    emb, hid = w1.shape
    W = 2 * C
    bands = jnp.linspace(1e-4, FILT_BANDS - 1, FILT_BANDS, dtype=F32)
    col = jnp.zeros((LANES, 1), F32).at[1:1 + FILT_BANDS, 0].set(bands)
    col = col.at[1 + FILT_BANDS:1 + 2 * FILT_BANDS, 0].set(bands)
    w1t = jnp.zeros((hid, LANES), F32).at[:, :emb].set(w1.T)
    per_dir = L // tp
    const = lambda i: (0, 0)
    vmem = (2 * _nbytes((W * SUB, LANES), F32) + 2 * _nbytes((W, LANES), F32) * 3
            + 6 * _nbytes((tch, tp), F32) + 8 * _nbytes((LANES, tp), F32))
    return pl.pallas_call(
        functools.partial(_filter_gen_kernel, tp=tp, tch=tch, L=L),
        grid=(N // tp,),
        in_specs=[pl.BlockSpec((LANES, 1), const),
                  pl.BlockSpec((hid, LANES), const), pl.BlockSpec((hid, 1), const),
                  pl.BlockSpec((hid, 1), const),
                  pl.BlockSpec((hid, hid), const), pl.BlockSpec((hid, 1), const),
                  pl.BlockSpec((hid, 1), const),
                  pl.BlockSpec((W, hid), lambda i: (i // per_dir, 0)),
                  pl.BlockSpec((W, 1), lambda i: (i // per_dir, 0))],
        out_specs=[pl.BlockSpec((1, W * SUB, LANES), lambda i: (i, 0, 0)),
                   pl.BlockSpec((W, 1), const)],
        out_shape=[jax.ShapeDtypeStruct((N // tp, W * SUB, LANES), F32),
                   jax.ShapeDtypeStruct((W, 1), F32)],
        compiler_params=_params(vmem, ("arbitrary",)),
        name="filter_gen",
    )(col, w1t, b1.reshape(hid, 1), f1.reshape(hid, 1), w2.T, b2.reshape(hid, 1),
      f2.reshape(hid, 1), w3.T, decay.reshape(2 * W, 1))


def _filter_spec_kernel(k_ref, s_ref, fk_ref, tw_ref, mb_ref, kf_ref, *, tcf, R):
    nb = k_ref.shape[0]

    def body(g, carry):
        r0 = pl.multiple_of(g * GROUP * SUB, GROUP * SUB)
        x = _gather_group(lambda i, j: k_ref[i, pl.ds(r0 + j * SUB, SUB), :], nb)
        a = _dot(fk_ref[...], x.astype(BF16))
        z = _forward(a, tw_ref[0], tw_ref[1], mb_ref[...])
        c0 = pl.program_id(0) * tcf + g * GROUP
        inv = jnp.concatenate(
            [jnp.broadcast_to(1.0 / s_ref[pl.ds(c0 + j, 1), :], (R, 1)) for j in range(GROUP)], axis=0)
        kf_ref[pl.ds(pl.multiple_of(g * GROUP * R, GROUP * R), GROUP * R), :] = (z * inv).astype(BF16)
        return carry

    lax.fori_loop(0, tcf // GROUP, body, 0)


def _filter_spec(k, s, tabs, *, tcf):
    nb, rows, _ = k.shape
    W = rows // SUB
    R = LANES
    vmem = (2 * _nbytes((nb, tcf * SUB, LANES), F32) + 2 * _nbytes((tcf * R, 2 * R), BF16)
            + 2 * _nbytes((W, LANES), F32) + 16 * _nbytes((2 * R, GROUP * R), F32))
    return pl.pallas_call(
        functools.partial(_filter_spec_kernel, tcf=tcf, R=R),
        grid=(W // tcf,),
        in_specs=[pl.BlockSpec((nb, tcf * SUB, LANES), lambda c: (0, c, 0)),
                  pl.BlockSpec((W, 1), lambda c: (0, 0)),
                  pl.BlockSpec(tabs["fk"].shape, lambda c: (0, 0)),
                  pl.BlockSpec(tabs["tw"].shape, lambda c: (0, 0, 0)),
                  pl.BlockSpec(tabs["mb"].shape, lambda c: (0, 0))],
        out_specs=pl.BlockSpec((tcf * R, 2 * R), lambda c: (c, 0)),
        out_shape=jax.ShapeDtypeStruct((W * R, 2 * R), BF16),
        compiler_params=_params(vmem, ("parallel",)),
        name="filter_spec",
    )(k, s, tabs["fk"], tabs["tw"], tabs["mb"])


def _hyena_kernel(v_ref, x1_ref, x2_ref, kf0_ref, kf1_ref, d_ref, f1_ref, i1_ref,
                  tw_ref, mb_ref, mbi_ref, o_ref, *, tcc, R):
    cb = pl.program_id(0)
    nb = v_ref.shape[3]
    H = R // 2

    def conv(x, kf):
        tr, ti = tw_ref[0], tw_ref[1]
        a = _dot(f1_ref[...], x.astype(BF16))
        z = _forward(a, tr, ti, mb_ref[...]).astype(BF16)
        zr, zi = z[:, :R], z[:, R:]
        kr, ki = kf[:, :R], kf[:, R:]
        y = jnp.concatenate([zr * kr - zi * ki, zr * ki + zi * kr], axis=1)
        br, bi = _to_lanes(_dot(y, mbi_ref[...]).astype(BF16))
        b = jnp.concatenate([br * tr + bi * ti, bi * tr - br * ti], axis=0)
        return _dot(i1_ref[...], b)

    def body(g, carry):
        r0 = pl.multiple_of(g * GROUP * SUB, GROUP * SUB)
        k0 = pl.multiple_of(g * GROUP * R, GROUP * R)

        def load(ref):
            halves = [_gather_group(lambda i, j, s=s: ref[0, 0, s, i, pl.ds(r0 + j * SUB, SUB), :], nb)
                      for s in range(2)]
            return jnp.concatenate(halves, axis=0)

        drow = cb * (tcc // GROUP) + g
        v = load(v_ref)
        z1 = load(x1_ref) * (conv(v, kf0_ref[pl.ds(k0, GROUP * R), :]) + d_ref[0, pl.ds(drow, 1), :] * v)
        u = load(x2_ref) * (conv(z1, kf1_ref[pl.ds(k0, GROUP * R), :]) + d_ref[1, pl.ds(drow, 1), :] * z1)
        for s in range(2):
            for i in range(nb):
                for j in range(GROUP):
                    o_ref[0, s, i, pl.ds(r0 + j * SUB, SUB), :] = (
                        u[s * H + i * SUB:s * H + (i + 1) * SUB, j * R:(j + 1) * R])
        return carry

    lax.fori_loop(0, tcc // GROUP, body, 0, unroll=4)


def _hyena(acts, kf, dl, tabs, *, tcc):
    _, P, _, NB, rows, _ = acts.shape
    C = rows // SUB
    R = LANES
    per = C // tcc
    act = lambda w: pl.BlockSpec((1, 1, 2, NB, tcc * SUB, LANES), lambda c, p, w=w: (w, p, 0, 0, c, 0))
    const2 = lambda c, p: (0, 0)
    const3 = lambda c, p: (0, 0, 0)
    vmem = (2 * 4 * _nbytes((2, NB, tcc * SUB, LANES), F32) + 2 * 2 * _nbytes((tcc * R, 2 * R), BF16)
            + 2 * _nbytes(dl.shape, F32) + 2 * _nbytes(tabs["tw"].shape, BF16)
            + 24 * _nbytes((2 * R, GROUP * R), F32))
    return pl.pallas_call(
        functools.partial(_hyena_kernel, tcc=tcc, R=R),
        grid=(per, P),
        in_specs=[act(0), act(1), act(2),
                  pl.BlockSpec((tcc * R, 2 * R), lambda c, p: (c, 0)),
                  pl.BlockSpec((tcc * R, 2 * R), lambda c, p: (per + c, 0)),
                  pl.BlockSpec(dl.shape, const3),
                  pl.BlockSpec(tabs["f1"].shape, const2),
                  pl.BlockSpec(tabs["i1"].shape, const2),
                  pl.BlockSpec(tabs["tw"].shape, const3),
                  pl.BlockSpec(tabs["mb"].shape, const2),
                  pl.BlockSpec(tabs["mbi"].shape, const2)],
        out_specs=pl.BlockSpec((1, 2, NB, tcc * SUB, LANES), lambda c, p: (p, 0, 0, c, 0)),
        out_shape=jax.ShapeDtypeStruct((P, 2, NB, rows, LANES), F32),
        compiler_params=_params(vmem, ("parallel", "parallel")),
        name="hyena",
    )(acts, acts, acts, kf, kf, dl, tabs["f1"], tabs["i1"], tabs["tw"], tabs["mb"], tabs["mbi"])


def _merge_kernel(u_ref, fz_ref, sfg_ref, shg_ref, gf_ref, gh_ref, x_ref, perm_ref, wfo_ref, who_ref,
                  wout_ref, gpost_ref, o_ref, *, C, D, qq):
    q0 = pl.program_id(2) * qq
    tp = qq * LANES
    nat = lambda ref: ref[0].reshape(2 * tp, ref.shape[-1])
    us = [u_ref[0, s, 0, pl.ds(q0 + q, C, stride=SUB), :].T for s in range(2) for q in range(qq)]
    u = jnp.concatenate(us, axis=0) * nat(shg_ref).astype(F32)
    yh = _dot(u.astype(BF16), who_ref[...])
    fz = jnp.concatenate([_dot(perm_ref[...], fz_ref[0, s].reshape(tp, C)) for s in range(2)], axis=0)
    yf = _dot((fz * nat(sfg_ref).astype(F32)).astype(BF16), wfo_ref[...])
    m = nat(gf_ref).astype(F32) * yf + nat(gh_ref).astype(F32) * yh
    out = _dot(m.astype(BF16), wout_ref[...])
    var = jnp.mean(out * out, axis=-1, keepdims=True)
    res = nat(x_ref) + out * lax.rsqrt(var + RMS_EPS) * gpost_ref[...]
    o_ref[0] = res.reshape(2, tp, D)


def _merge(u, fz, sfg, shg, gf, gh, x, perm, wfo, who, wout, g_post):
    P, _, NB, rows, _ = u.shape
    C = rows // SUB
    B, L, D = x.shape
    tp = FA * TILE
    qq = tp // LANES
    pv = lambda t: t.reshape(P, 2, L, t.shape[-1])
    nat = lambda w: pl.BlockSpec((1, 2, tp, w), lambda p, i, q: (p, 0, i * (SUB // qq) + q, 0))
    const = lambda p, i, q: (0, 0)
    single = dict(pipeline_mode=pl.Buffered(1))
    weights = [perm, wfo, who, wout]
    vmem = (2 * _nbytes((2, rows, LANES), F32) + 2 * 5 * _nbytes((2, tp, D), BF16)
            + 4 * _nbytes((2, tp, D), F32) + sum(_nbytes(w.shape, BF16) for w in weights)
            + 8 * _nbytes((2 * tp, D), F32))
    out = pl.pallas_call(
        functools.partial(_merge_kernel, C=C, D=D, qq=qq),
        grid=(P, NB, SUB // qq),
        in_specs=[pl.BlockSpec((1, 2, 1, rows, LANES), lambda p, i, q: (p, 0, i, 0, 0)),
                  pl.BlockSpec((1, 2, FA, TILE, C), lambda p, i, q: (p, 0, 0, i * (SUB // qq) + q, 0)),
                  nat(C), nat(C), nat(D), nat(D), nat(D)]
                 + [pl.BlockSpec(w.shape, const, **single) for w in weights]
                 + [pl.BlockSpec((1, D), const)],
        out_specs=nat(D),
        out_shape=jax.ShapeDtypeStruct((P, 2, L, D), F32),
        compiler_params=_params(vmem, ("parallel", "parallel", "arbitrary")),
        name="merge",
    )(u, fz.reshape((P, 2) + fz.shape[1:]), pv(sfg), pv(shg), pv(gf), pv(gh), pv(x), *weights,
      g_post.reshape(1, D))
    return out.reshape(B, L, D)


def _tile(n, want):
    t = min(n, want)
    while n % t:
        t //= 2
    return t


def _hyena_filters(L, C, tabs, w1, b1, f1, w2, b2, f2, w3, decay):
    k, s = _filter_gen(L, C, w1, b1, f1, w2, b2, f2, w3, decay, tch=_tile(2 * C, 512))
    return _filter_spec(k, s, tabs, tcf=_tile(2 * C, 32))


def _encoder_layer(x, kf, tabs, cd, dl, g_pre, w_nat, w_cm, w_short, b_merge, wfo, who, wout, g_post):
    B, L, D = x.shape
    C = D
    FB = L // FA
    z, sfg, shg, gf, gh = _inproj_nat(x, g_pre, w_nat, b_merge, cd, tm=_tile(L, 512))
    acts = _inproj_cm(x, g_pre, w_cm, w_short, tch=_tile(C, 1024))

    af = _fourier_a(z.reshape(B, 2, FA, FB // TILE, TILE, C), tabs["fa"], tt=8)
    scale = 1.0 / math.sqrt(L * (C // F_GROUPS))
    fz = _fourier_b(af.reshape(B, 2, FA, FB, C), tabs["fb"], scale=scale, kb=4)

    u = _hyena(acts, kf, dl, tabs, tcc=_tile(C, 32))
    return _merge(u, fz, sfg, shg, gf, gh, x, tabs["perm"], wfo, who, wout, g_post)


def kernel(x_prompt, x_sample, g_pre, w_in, w_short, filt_w1, filt_b1, filt_freq1, filt_w2,
           filt_b2, filt_freq2, filt_w3, filt_decay, hyena_d, w_fourier_out, w_hyena_out,
           b_merge, w_out, g_post):
    depth = g_pre.shape[0]
    L, D = x_prompt.shape[1], x_prompt.shape[2]
    assert x_sample.shape[1:] == (L, D)
    C = D
    R = LANES
    assert R * R == 2 * L and C % (GROUP * F_GROUPS) == 0
    assert x_prompt.shape[0] % 2 == 0 and x_sample.shape[0] % 2 == 0
    tabs = _tables(R)
    cd = _channel_dft(C // F_GROUPS)
    ys = [x_prompt, x_sample]
    for i in range(depth):
        w_nat = jnp.concatenate([w_in[i, :, :2 * C], w_in[i, :, 5 * C:]], axis=1).astype(BF16)
        w_cm = w_in[i, :, 2 * C:5 * C].astype(BF16)
        dl = jnp.repeat(hyena_d[i], R, axis=-1).reshape(2, C // GROUP, GROUP * R)
        kf = _hyena_filters(L, C, tabs, filt_w1[i], filt_b1[i], filt_freq1[i], filt_w2[i],
                            filt_b2[i], filt_freq2[i], filt_w3[i], filt_decay[i])
        args = (kf, tabs, cd, dl, g_pre[i], w_nat, w_cm, w_short[i], b_merge[i],
                w_fourier_out[i].astype(BF16), w_hyena_out[i].astype(BF16),
                w_out[i].astype(BF16), g_post[i])
        ys = [_encoder_layer(y, *args) for y in ys]
    return tuple(ys)
```

```python
import functools
import math

import jax
import jax.numpy as jnp
from jax import lax
from jax.experimental import pallas as pl
from jax.experimental.pallas import tpu as pltpu

BF16 = jnp.bfloat16
F32 = jnp.float32

F_GROUPS = 4
RMS_EPS = 1e-6
FILT_BANDS = 16
TILE = 16
HALO = TILE
FA = 16
LANES = 128
SUB = 8
MXU_COLS = 256
GROUP = 32
V7X_VMEM_BYTES = 64 * 1024 * 1024
VMEM_CAP = V7X_VMEM_BYTES - 8 * 1024 * 1024
HIGHEST = lax.Precision.HIGHEST


def _dot(a, b):
    return jnp.dot(a, b, preferred_element_type=F32)


def _params(vmem_bytes, semantics):
    limit = min(VMEM_CAP, vmem_bytes + 8 * 1024 * 1024)
    return pltpu.CompilerParams(dimension_semantics=semantics, vmem_limit_bytes=int(limit))


def _nbytes(shape, dtype):
    return math.prod(shape) * jnp.dtype(dtype).itemsize


def _silu(x):
    return x * jax.nn.sigmoid(x)


def _cis(num, den):
    ang = (num % den).astype(F32) * (2.0 * math.pi / den)
    return jnp.cos(ang), -jnp.sin(ang)


def _block(rr, ri):
    top = jnp.concatenate([rr, -ri], axis=-1)
    bot = jnp.concatenate([ri, rr], axis=-1)
    return jnp.concatenate([top, bot], axis=-2)


def _tables(R):
    N = R * R
    Lh = N // 2
    H = R // 2
    i32 = jnp.int32
    a = jnp.arange(R, dtype=i32)
    wr, wi = _cis(a[:, None] * a[None, :], R)
    f1 = _block(wr[:, :H], wi[:, :H])
    i1 = _block(wr[:, :H].T, -wi[:, :H].T) * (1.0 / N)
    fk = jnp.concatenate([wr, wi], axis=0)
    mb = _block(wr, -wi)
    mbi = _block(wr, wi)
    tr, ti = _cis(a[:, None] * a[None, :], N)
    tw = jnp.stack([jnp.tile(tr, (1, GROUP)), jnp.tile(ti, (1, GROUP))])
    FB = Lh // FA
    half = FA * TILE
    t = jnp.arange(FB // TILE, dtype=i32)[:, None, None]
    row = jnp.arange(2 * half, dtype=i32)[None, :, None]
    col = jnp.arange(2 * half, dtype=i32)[None, None, :]
    k1, j = (row // TILE) % FA, row % TILE
    n1, jc = (col // TILE) % FA, col % TILE
    vr, vi = _cis(k1 * (FB * n1 + TILE * t + j), Lh)
    ro, co = row // half, col // half
    fa = jnp.where(j == jc, jnp.where(ro == co, vr, jnp.where(ro > co, vi, -vi)), 0.0)
    b = jnp.arange(FB, dtype=i32)
    br, bi = _cis(b[:, None] * b[None, :], FB)
    fb = jnp.concatenate([br, -bi], axis=1)
    perm = jnp.eye(FA * TILE, dtype=F32).reshape(FA, TILE, FA * TILE).swapaxes(0, 1).reshape(FA * TILE, FA * TILE)
    cast = lambda t: t.astype(BF16)
    return dict(f1=cast(f1), i1=cast(i1), fk=cast(fk), mb=cast(mb), mbi=cast(mbi), tw=cast(tw),
                fa=cast(fa), fb=cast(fb), perm=cast(perm))


def _channel_dft(gd):
    a = jnp.arange(gd, dtype=jnp.int32)
    cr, ci = _cis(a[:, None] * a[None, :], gd)
    return jnp.concatenate([cr, ci], axis=1).astype(BF16)


def _inproj_nat_kernel(x_ref, gpre_ref, w_ref, bm_ref, cd_ref,
                       z_ref, sfg_ref, shg_ref, gf_ref, gh_ref, *, C, D):
    xx = x_ref[0]
    ms = jnp.mean(xx * xx, axis=-1, keepdims=True)
    xm = (xx * lax.rsqrt(ms + RMS_EPS) * gpre_ref[...]).astype(BF16)
    wf_ref, whg_ref, wm_ref = w_ref.at[:, :2 * C], w_ref.at[:, 2 * C:3 * C], w_ref.at[:, 3 * C:]
    shg_ref[0] = _silu(_dot(xm, whg_ref[...])).astype(BF16)
    pf = _dot(xm, wf_ref[...])
    sfg_ref[0] = _silu(pf[:, C:]).astype(BF16)
    fv = pf[:, :C].astype(BF16)
    gd = C // F_GROUPS
    for gi in range(F_GROUPS):
        zz = _dot(fv[:, gi * gd:(gi + 1) * gd], cd_ref[...])
        z_ref[0, 0, :, gi * gd:(gi + 1) * gd] = zz[:, :gd].astype(BF16)
        z_ref[0, 1, :, gi * gd:(gi + 1) * gd] = zz[:, gd:].astype(BF16)
    gate = jax.nn.sigmoid(_dot(xm, wm_ref[...]) + bm_ref[...])
    gf_ref[0] = gate[:, :D].astype(BF16)
    gh_ref[0] = gate[:, D:].astype(BF16)


def _inproj_nat(x, g_pre, w, b_merge, cd, *, tm):
    B, L, D = x.shape
    C = D
    const = lambda b, i: (0, 0)
    row = pl.BlockSpec((1, tm, C), lambda b, i: (b, i, 0))
    act = jax.ShapeDtypeStruct((B, L, C), BF16)
    vmem = (2 * _nbytes((tm, D), F32) + 2 * 6 * _nbytes((tm, C), BF16)
            + _nbytes(w.shape, BF16) + 5 * _nbytes((tm, 2 * C), F32))
    return pl.pallas_call(
        functools.partial(_inproj_nat_kernel, C=C, D=D),
        grid=(B, L // tm),
        in_specs=[pl.BlockSpec((1, tm, D), lambda b, i: (b, i, 0)),
                  pl.BlockSpec((1, D), const),
                  pl.BlockSpec(w.shape, const, pipeline_mode=pl.Buffered(1)),
                  pl.BlockSpec((1, 2 * D), const),
                  pl.BlockSpec(cd.shape, const)],
        out_specs=[pl.BlockSpec((1, 2, tm, C), lambda b, i: (b, 0, i, 0)), row, row, row, row],
        out_shape=[jax.ShapeDtypeStruct((B, 2, L, C), BF16), act, act, act, act],
        compiler_params=_params(vmem, ("parallel", "parallel")),
        name="inproj_nat",
    )(x, g_pre.reshape(1, D), w, b_merge.reshape(1, 2 * D), cd)


def _inproj_cm_kernel(x_ref, xp_ref, xn_ref, gpre_ref, w_ref, wsh_ref, o_ref, xe_ref, *pe_refs,
                      tp, tch):
    i = pl.program_id(1)
    ch = pl.program_id(2)
    last = pl.num_programs(1) - 1
    gain = gpre_ref[...]

    def norm(xx):
        ms = jnp.mean(xx * xx, axis=-1, keepdims=True)
        return xx * lax.rsqrt(ms + RMS_EPS) * gain

    @pl.when(ch == 0)
    def _():
        for s in range(2):
            xe_ref[s, :HALO] = (norm(xp_ref[0, s]) * (i > 0).astype(F32)).astype(BF16)
            xe_ref[s, HALO:HALO + tp] = norm(x_ref[0, s]).astype(BF16)
            xe_ref[s, HALO + tp:] = (norm(xn_ref[0, s]) * (i < last).astype(F32)).astype(BF16)

    nh = tch // MXU_COLS
    for s in range(2):
        for h in range(nh):
            cols = slice(h * MXU_COLS, (h + 1) * MXU_COLS)
            taps = wsh_ref[:, cols]
            pe_ref = pe_refs[s * nh + h]
            pe_ref[...] = _dot(xe_ref[s], w_ref[:, cols])
            conv = (pe_ref[pl.ds(HALO - 1, tp), :] * taps[0:1]
                    + pe_ref[pl.ds(HALO, tp), :] * taps[1:2]
                    + pe_ref[pl.ds(HALO + 1, tp), :] * taps[2:3])
            y = conv.T
            for q in range(tp // LANES):
                o_ref[0, 0, s, 0, pl.ds(h * MXU_COLS * SUB + q, MXU_COLS, stride=SUB), :] = (
                    y[:, q * LANES:(q + 1) * LANES])


def _inproj_cm(x, g_pre, w, w_short, *, tch):
    B, L, D = x.shape
    C = D
    P = B // 2
    tp = SUB * LANES
    NB = L // tp
    n_out = w_short.shape[1] // C
    per = C // tch
    nh = tp // HALO
    te = tp + 2 * HALO
    xv = x.reshape(P, 2, L, D)
    vmem = (2 * _nbytes((2, tp, D), F32) + _nbytes((2, te, D), BF16) + _nbytes((2, te, tch), F32)
            + 2 * _nbytes((D, tch), BF16) + 2 * _nbytes((2, tch * SUB, LANES), F32)
            + 8 * _nbytes((tch, tp), F32))
    return pl.pallas_call(
        functools.partial(_inproj_cm_kernel, tp=tp, tch=tch),
        grid=(P, NB, n_out * per),
        in_specs=[pl.BlockSpec((1, 2, tp, D), lambda p, i, c: (p, 0, i, 0)),
                  pl.BlockSpec((1, 2, HALO, D), lambda p, i, c: (p, 0, jnp.maximum(i * nh - 1, 0), 0)),
                  pl.BlockSpec((1, 2, HALO, D),
                               lambda p, i, c: (p, 0, jnp.minimum((i + 1) * nh, L // HALO - 1), 0)),
                  pl.BlockSpec((1, D), lambda p, i, c: (0, 0)),
                  pl.BlockSpec((D, tch), lambda p, i, c: (0, c)),
                  pl.BlockSpec((3, tch), lambda p, i, c: (0, c))],
        out_specs=pl.BlockSpec((1, 1, 2, 1, tch * SUB, LANES),
                               lambda p, i, c: (c // per, p, 0, i, c % per, 0)),
        out_shape=jax.ShapeDtypeStruct((n_out, P, 2, NB, C * SUB, LANES), F32),
        scratch_shapes=[pltpu.VMEM((2, te, D), BF16)]
                       + [pltpu.VMEM((te, MXU_COLS), F32)] * (2 * tch // MXU_COLS),
        compiler_params=_params(vmem, ("parallel", "parallel", "arbitrary")),
        name="inproj_cm",
    )(xv, xv, xv, g_pre.reshape(1, D), w, w_short)


def _fourier_a_kernel(z_ref, fa_ref, a_ref, *, tt):
    for tl in range(tt):
        x = z_ref[0, :, :, tl]
        y = _dot(fa_ref[tl], x.reshape(-1, x.shape[-1]))
        a_ref[0, :, :, tl] = y.astype(BF16).reshape(x.shape)


def _fourier_a(z, fa, *, tt):
    B, _, _, T, _, C = z.shape
    blk = (1, 2, FA, tt, TILE, C)
    vmem = 4 * _nbytes(blk, BF16) + 2 * _nbytes((tt,) + fa.shape[1:], BF16) + 4 * _nbytes((2 * FA * TILE, C), F32)
    return pl.pallas_call(
        functools.partial(_fourier_a_kernel, tt=tt),
        grid=(T // tt, B),
        in_specs=[pl.BlockSpec(blk, lambda t, b: (b, 0, 0, t, 0, 0)),
                  pl.BlockSpec((tt,) + fa.shape[1:], lambda t, b: (t, 0, 0))],
        out_specs=pl.BlockSpec(blk, lambda t, b: (b, 0, 0, t, 0, 0)),
        out_shape=jax.ShapeDtypeStruct(z.shape, BF16),
        compiler_params=_params(vmem, ("parallel", "parallel")),
        name="fourier_a",
    )(z, fa)


def _fourier_b_kernel(a_ref, fb_ref, o_ref, *, scale, kb):
    for q in range(kb):
        a = a_ref[0, :, q]
        fz = _dot(fb_ref[...], a.reshape(-1, a.shape[-1])) * scale
        o_ref[0, q] = fz.astype(BF16)


def _fourier_b(a, fb, *, scale, kb=2):
    B, _, _, FB, C = a.shape
    vmem = 2 * 3 * kb * _nbytes((FB, C), BF16) + 2 * _nbytes(fb.shape, BF16) + 4 * _nbytes((FB, C), F32)
    return pl.pallas_call(
        functools.partial(_fourier_b_kernel, scale=scale, kb=kb),
        grid=(B, FA // kb),
        in_specs=[pl.BlockSpec((1, 2, kb, FB, C), lambda b, k: (b, 0, k, 0, 0)),
                  pl.BlockSpec(fb.shape, lambda b, k: (0, 0))],
        out_specs=pl.BlockSpec((1, kb, FB, C), lambda b, k: (b, k, 0, 0)),
        out_shape=jax.ShapeDtypeStruct((B, FA, FB, C), BF16),
        compiler_params=_params(vmem, ("parallel", "parallel")),
        name="fourier_b",
    )(a, fb)


def _gather_group(load, nb):
    rows = [jnp.concatenate([load(i, j) for j in range(GROUP)], axis=1) for i in range(nb)]
    return jnp.concatenate(rows, axis=0)


def _to_rows(ar, ai):
    R = ar.shape[0]
    parts = [jnp.concatenate([ar[:, j * R:(j + 1) * R], ai[:, j * R:(j + 1) * R]], axis=1)
             for j in range(GROUP)]
    return jnp.concatenate(parts, axis=0)


def _to_lanes(z):
    R = z.shape[1] // 2
    zr = jnp.concatenate([z[j * R:(j + 1) * R, :R] for j in range(GROUP)], axis=1)
    zi = jnp.concatenate([z[j * R:(j + 1) * R, R:] for j in range(GROUP)], axis=1)
    return zr, zi


def _forward(a, tr, ti, mb):
    R = a.shape[0] // 2
    ar, ai = a[:R].astype(BF16), a[R:].astype(BF16)
    return _dot(_to_rows(ar * tr - ai * ti, ar * ti + ai * tr), mb)


def _filter_gen_kernel(bands_ref, w1_ref, b1_ref, f1_ref, w2_ref, b2_ref, f2_ref, w3_ref,
                       dec_ref, k_ref, s_ref, *, tp, tch, L):
    i = pl.program_id(0)
    n = i * tp + lax.broadcasted_iota(jnp.int32, (1, tp), 1)
    t = jnp.where(n < L, n, 2 * L - 1 - n).astype(F32)
    tnorm = t / (L - 1)
    ang = (2.0 * math.pi * t) / L
    row = lax.broadcasted_iota(jnp.int32, (LANES, tp), 0)
    arg = bands_ref[...] * ang
    feats = jnp.where(row == 0, tnorm,
                      jnp.where(row <= FILT_BANDS, jnp.cos(arg),
                                jnp.where(row <= 2 * FILT_BANDS, -jnp.sin(arg), 0.0)))
    hdot = lambda a, b: jnp.dot(a, b, precision=HIGHEST, preferred_element_type=F32)
    h = jnp.sin(f1_ref[...] * (hdot(w1_ref[...], feats) + b1_ref[...]))
    hb = jnp.sin(f2_ref[...] * (hdot(w2_ref[...], h) + b2_ref[...])).astype(BF16)

    @pl.when(i == 0)
    def _():
        s_ref[...] = jnp.zeros_like(s_ref)

    for c in range(w3_ref.shape[0] // tch):
        rows = pl.ds(c * tch, tch)
        hc = _dot(w3_ref[rows, :].astype(BF16), hb) * jnp.exp(-tnorm * jnp.abs(dec_ref[rows, :]))
        s_ref[rows, :] += jnp.sum(jnp.abs(hc), axis=1, keepdims=True)
        hc = jnp.where(n == L, 0.0, hc)
        for q in range(tp // LANES):
            k_ref[0, pl.ds(c * tch * SUB + q, tch, stride=SUB), :] = hc[:, q * LANES:(q + 1) * LANES]


def _filter_gen(L, C, w1, b1, f1, w2, b2, f2, w3, decay, *, tch):
    N = 2 * L
    tp = SUB * LANES
    emb, hid = w1.shape
    W = 2 * C
    bands = jnp.linspace(1e-4, FILT_BANDS - 1, FILT_BANDS, dtype=F32)
    col = jnp.zeros((LANES, 1), F32).at[1:1 + FILT_BANDS, 0].set(bands)
    col = col.at[1 + FILT_BANDS:1 + 2 * FILT_BANDS, 0].set(bands)
    w1t = jnp.zeros((hid, LANES), F32).at[:, :emb].set(w1.T)
    per_dir = L // tp
    const = lambda i: (0, 0)
    vmem = (2 * _nbytes((W * SUB, LANES), F32) + 2 * _nbytes((W, LANES), F32) * 3
            + 6 * _nbytes((tch, tp), F32) + 8 * _nbytes((LANES, tp), F32))
    return pl.pallas_call(
        functools.partial(_filter_gen_kernel, tp=tp, tch=tch, L=L),
        grid=(N // tp,),
        in_specs=[pl.BlockSpec((LANES, 1), const),
                  pl.BlockSpec((hid, LANES), const), pl.BlockSpec((hid, 1), const),
                  pl.BlockSpec((hid, 1), const),
                  pl.BlockSpec((hid, hid), const), pl.BlockSpec((hid, 1), const),
                  pl.BlockSpec((hid, 1), const),
                  pl.BlockSpec((W, hid), lambda i: (i // per_dir, 0)),
                  pl.BlockSpec((W, 1), lambda i: (i // per_dir, 0))],
        out_specs=[pl.BlockSpec((1, W * SUB, LANES), lambda i: (i, 0, 0)),
                   pl.BlockSpec((W, 1), const)],
        out_shape=[jax.ShapeDtypeStruct((N // tp, W * SUB, LANES), F32),
                   jax.ShapeDtypeStruct((W, 1), F32)],
        compiler_params=_params(vmem, ("arbitrary",)),
        name="filter_gen",
    )(col, w1t, b1.reshape(hid, 1), f1.reshape(hid, 1), w2.T, b2.reshape(hid, 1),
      f2.reshape(hid, 1), w3.T, decay.reshape(2 * W, 1))


def _filter_spec_kernel(k_ref, s_ref, fk_ref, tw_ref, mb_ref, kf_ref, *, tcf, R):
    nb = k_ref.shape[0]

    def body(g, carry):
        r0 = pl.multiple_of(g * GROUP * SUB, GROUP * SUB)
        x = _gather_group(lambda i, j: k_ref[i, pl.ds(r0 + j * SUB, SUB), :], nb)
        a = _dot(fk_ref[...], x.astype(BF16))
        z = _forward(a, tw_ref[0], tw_ref[1], mb_ref[...])
        c0 = pl.program_id(0) * tcf + g * GROUP
        inv = jnp.concatenate(
            [jnp.broadcast_to(1.0 / s_ref[pl.ds(c0 + j, 1), :], (R, 1)) for j in range(GROUP)], axis=0)
        kf_ref[pl.ds(pl.multiple_of(g * GROUP * R, GROUP * R), GROUP * R), :] = (z * inv).astype(BF16)
        return carry

    lax.fori_loop(0, tcf // GROUP, body, 0)


def _filter_spec(k, s, tabs, *, tcf):
    nb, rows, _ = k.shape
    W = rows // SUB
    R = LANES
    vmem = (2 * _nbytes((nb, tcf * SUB, LANES), F32) + 2 * _nbytes((tcf * R, 2 * R), BF16)
            + 2 * _nbytes((W, LANES), F32) + 16 * _nbytes((2 * R, GROUP * R), F32))
    return pl.pallas_call(
        functools.partial(_filter_spec_kernel, tcf=tcf, R=R),
        grid=(W // tcf,),
        in_specs=[pl.BlockSpec((nb, tcf * SUB, LANES), lambda c: (0, c, 0)),
                  pl.BlockSpec((W, 1), lambda c: (0, 0)),
                  pl.BlockSpec(tabs["fk"].shape, lambda c: (0, 0)),
                  pl.BlockSpec(tabs["tw"].shape, lambda c: (0, 0, 0)),
                  pl.BlockSpec(tabs["mb"].shape, lambda c: (0, 0))],
        out_specs=pl.BlockSpec((tcf * R, 2 * R), lambda c: (c, 0)),
        out_shape=jax.ShapeDtypeStruct((W * R, 2 * R), BF16),
        compiler_params=_params(vmem, ("parallel",)),
        name="filter_spec",
    )(k, s, tabs["fk"], tabs["tw"], tabs["mb"])


def _hyena_kernel(v_ref, x1_ref, x2_ref, kf0_ref, kf1_ref, d_ref, f1_ref, i1_ref,
                  tw_ref, mb_ref, mbi_ref, o_ref, *, tcc, R):
    cb = pl.program_id(0)
    nb = v_ref.shape[3]
    H = R // 2

    def conv(x, kf):
        tr, ti = tw_ref[0], tw_ref[1]
        a = _dot(f1_ref[...], x.astype(BF16))
        z = _forward(a, tr, ti, mb_ref[...]).astype(BF16)
        zr, zi = z[:, :R], z[:, R:]
        kr, ki = kf[:, :R], kf[:, R:]
        y = jnp.concatenate([zr * kr - zi * ki, zr * ki + zi * kr], axis=1)
        br, bi = _to_lanes(_dot(y, mbi_ref[...]).astype(BF16))
        b = jnp.concatenate([br * tr + bi * ti, bi * tr - br * ti], axis=0)
        return _dot(i1_ref[...], b)

    def body(g, carry):
        r0 = pl.multiple_of(g * GROUP * SUB, GROUP * SUB)
        k0 = pl.multiple_of(g * GROUP * R, GROUP * R)

        def load(ref):
            halves = [_gather_group(lambda i, j, s=s: ref[0, 0, s, i, pl.ds(r0 + j * SUB, SUB), :], nb)
                      for s in range(2)]
            return jnp.concatenate(halves, axis=0)

        drow = cb * (tcc // GROUP) + g
        v = load(v_ref)
        z1 = load(x1_ref) * (conv(v, kf0_ref[pl.ds(k0, GROUP * R), :]) + d_ref[0, pl.ds(drow, 1), :] * v)
        u = load(x2_ref) * (conv(z1, kf1_ref[pl.ds(k0, GROUP * R), :]) + d_ref[1, pl.ds(drow, 1), :] * z1)
        for s in range(2):
            for i in range(nb):
                for j in range(GROUP):
                    o_ref[0, s, i, pl.ds(r0 + j * SUB, SUB), :] = (
                        u[s * H + i * SUB:s * H + (i + 1) * SUB, j * R:(j + 1) * R])
        return carry

    lax.fori_loop(0, tcc // GROUP, body, 0, unroll=4)


def _hyena(acts, kf, dl, tabs, *, tcc):
    _, P, _, NB, rows, _ = acts.shape
    C = rows // SUB
    R = LANES
    per = C // tcc
    act = lambda w: pl.BlockSpec((1, 1, 2, NB, tcc * SUB, LANES), lambda c, p, w=w: (w, p, 0, 0, c, 0))
    const2 = lambda c, p: (0, 0)
    const3 = lambda c, p: (0, 0, 0)
    vmem = (2 * 4 * _nbytes((2, NB, tcc * SUB, LANES), F32) + 2 * 2 * _nbytes((tcc * R, 2 * R), BF16)
            + 2 * _nbytes(dl.shape, F32) + 2 * _nbytes(tabs["tw"].shape, BF16)
            + 24 * _nbytes((2 * R, GROUP * R), F32))
    return pl.pallas_call(
        functools.partial(_hyena_kernel, tcc=tcc, R=R),
        grid=(per, P),
        in_specs=[act(0), act(1), act(2),
                  pl.BlockSpec((tcc * R, 2 * R), lambda c, p: (c, 0)),
                  pl.BlockSpec((tcc * R, 2 * R), lambda c, p: (per + c, 0)),
                  pl.BlockSpec(dl.shape, const3),
                  pl.BlockSpec(tabs["f1"].shape, const2),
                  pl.BlockSpec(tabs["i1"].shape, const2),
                  pl.BlockSpec(tabs["tw"].shape, const3),
                  pl.BlockSpec(tabs["mb"].shape, const2),
                  pl.BlockSpec(tabs["mbi"].shape, const2)],
        out_specs=pl.BlockSpec((1, 2, NB, tcc * SUB, LANES), lambda c, p: (p, 0, 0, c, 0)),
        out_shape=jax.ShapeDtypeStruct((P, 2, NB, rows, LANES), F32),
        compiler_params=_params(vmem, ("parallel", "parallel")),
        name="hyena",
    )(acts, acts, acts, kf, kf, dl, tabs["f1"], tabs["i1"], tabs["tw"], tabs["mb"], tabs["mbi"])


def _merge_kernel(u_ref, fz_ref, sfg_ref, shg_ref, gf_ref, gh_ref, x_ref, perm_ref, wfo_ref, who_ref,
                  wout_ref, gpost_ref, o_ref, *, C, D, qq):
    q0 = pl.program_id(2) * qq
    tp = qq * LANES
    nat = lambda ref: ref[0].reshape(2 * tp, ref.shape[-1])
    us = [u_ref[0, s, 0, pl.ds(q0 + q, C, stride=SUB), :].T for s in range(2) for q in range(qq)]
    u = jnp.concatenate(us, axis=0) * nat(shg_ref).astype(F32)
    yh = _dot(u.astype(BF16), who_ref[...])
    fz = jnp.concatenate([_dot(perm_ref[...], fz_ref[0, s].reshape(tp, C)) for s in range(2)], axis=0)
    yf = _dot((fz * nat(sfg_ref).astype(F32)).astype(BF16), wfo_ref[...])
    m = nat(gf_ref).astype(F32) * yf + nat(gh_ref).astype(F32) * yh
    out = _dot(m.astype(BF16), wout_ref[...])
    var = jnp.mean(out * out, axis=-1, keepdims=True)
    res = nat(x_ref) + out * lax.rsqrt(var + RMS_EPS) * gpost_ref[...]
    o_ref[0] = res.reshape(2, tp, D)


def _merge(u, fz, sfg, shg, gf, gh, x, perm, wfo, who, wout, g_post):
    P, _, NB, rows, _ = u.shape
    C = rows // SUB
    B, L, D = x.shape
    tp = FA * TILE
    qq = tp // LANES
    pv = lambda t: t.reshape(P, 2, L, t.shape[-1])
    nat = lambda w: pl.BlockSpec((1, 2, tp, w), lambda p, i, q: (p, 0, i * (SUB // qq) + q, 0))
    const = lambda p, i, q: (0, 0)
    single = dict(pipeline_mode=pl.Buffered(1))
    weights = [perm, wfo, who, wout]
    vmem = (2 * _nbytes((2, rows, LANES), F32) + 2 * 5 * _nbytes((2, tp, D), BF16)
            + 4 * _nbytes((2, tp, D), F32) + sum(_nbytes(w.shape, BF16) for w in weights)
            + 8 * _nbytes((2 * tp, D), F32))
    out = pl.pallas_call(
        functools.partial(_merge_kernel, C=C, D=D, qq=qq),
        grid=(P, NB, SUB // qq),
        in_specs=[pl.BlockSpec((1, 2, 1, rows, LANES), lambda p, i, q: (p, 0, i, 0, 0)),
                  pl.BlockSpec((1, 2, FA, TILE, C), lambda p, i, q: (p, 0, 0, i * (SUB // qq) + q, 0)),
                  nat(C), nat(C), nat(D), nat(D), nat(D)]
                 + [pl.BlockSpec(w.shape, const, **single) for w in weights]
                 + [pl.BlockSpec((1, D), const)],
        out_specs=nat(D),
        out_shape=jax.ShapeDtypeStruct((P, 2, L, D), F32),
        compiler_params=_params(vmem, ("parallel", "parallel", "arbitrary")),
        name="merge",
    )(u, fz.reshape((P, 2) + fz.shape[1:]), pv(sfg), pv(shg), pv(gf), pv(gh), pv(x), *weights,
      g_post.reshape(1, D))
    return out.reshape(B, L, D)


def _tile(n, want):
    t = min(n, want)
    while n % t:
        t //= 2
    return t


def _hyena_filters(L, C, tabs, w1, b1, f1, w2, b2, f2, w3, decay):
    k, s = _filter_gen(L, C, w1, b1, f1, w2, b2, f2, w3, decay, tch=_tile(2 * C, 512))
    return _filter_spec(k, s, tabs, tcf=_tile(2 * C, 32))


def _encoder_layer(x, kf, tabs, cd, dl, g_pre, w_nat, w_cm, w_short, b_merge, wfo, who, wout, g_post):
    B, L, D = x.shape
    C = D
    FB = L // FA
    z, sfg, shg, gf, gh = _inproj_nat(x, g_pre, w_nat, b_merge, cd, tm=_tile(L, 512))
    acts = _inproj_cm(x, g_pre, w_cm, w_short, tch=_tile(C, 1024))

    af = _fourier_a(z.reshape(B, 2, FA, FB // TILE, TILE, C), tabs["fa"], tt=8)
    scale = 1.0 / math.sqrt(L * (C // F_GROUPS))
    fz = _fourier_b(af.reshape(B, 2, FA, FB, C), tabs["fb"], scale=scale, kb=4)

    u = _hyena(acts, kf, dl, tabs, tcc=_tile(C, 32))
    return _merge(u, fz, sfg, shg, gf, gh, x, tabs["perm"], wfo, who, wout, g_post)


def kernel(x_prompt, x_sample, g_pre, w_in, w_short, filt_w1, filt_b1, filt_freq1, filt_w2,
           filt_b2, filt_freq2, filt_w3, filt_decay, hyena_d, w_fourier_out, w_hyena_out,
           b_merge, w_out, g_post):
    depth = g_pre.shape[0]
    L, D = x_prompt.shape[1], x_prompt.shape[2]
    assert x_sample.shape[1:] == (L, D)
    C = D
    R = LANES
    assert R * R == 2 * L and C % (GROUP * F_GROUPS) == 0
    assert x_prompt.shape[0] % 2 == 0 and x_sample.shape[0] % 2 == 0
    tabs = _tables(R)
    cd = _channel_dft(C // F_GROUPS)
    ys = [x_prompt, x_sample]
    for i in range(depth):
        w_nat = jnp.concatenate([w_in[i, :, :2 * C], w_in[i, :, 5 * C:]], axis=1).astype(BF16)
        w_cm = w_in[i, :, 2 * C:5 * C].astype(BF16)
        dl = jnp.repeat(hyena_d[i], R, axis=-1).reshape(2, C // GROUP, GROUP * R)
        kf = _hyena_filters(L, C, tabs, filt_w1[i], filt_b1[i], filt_freq1[i], filt_w2[i],
                            filt_b2[i], filt_freq2[i], filt_w3[i], filt_decay[i])
        args = (kf, tabs, cd, dl, g_pre[i], w_nat, w_cm, w_short[i], b_merge[i],
                w_fourier_out[i].astype(BF16), w_hyena_out[i].astype(BF16),
                w_out[i].astype(BF16), g_post[i])
        ys = [_encoder_layer(y, *args) for y in ys]
    return tuple(ys)
```
